```python
import math
import jax, jax.numpy as jnp
from jax import lax
import numpy as np

D_MODEL = 2048
BATCH = 4
SEQ = 8192
DEPTH = 4

CHUNK = 64
N_META = 16
Q_BLOCK = 128
MLA_HEADS = 16
QK_NOPE = 128
QK_ROPE = 64
V_DIM = 128
KV_RANK = 512
ROPE_BASE = 10000.0
MLA_SCALE = (QK_NOPE + QK_ROPE) ** -0.5
CONV_DIM = 2048
CONV_K = 31
SSM_DIM = 2048
SSM_GROUP = 16
SSM_GROUPS = SSM_DIM // SSM_GROUP
SSM_STATE = 64
FFN_DIM = 256 * ((8 * D_MODEL + 3 * 256 - 1) // (3 * 256))
DEEPNORM_ALPHA = (2 * DEPTH) ** 0.25
DEEPNORM_BETA = (8 * DEPTH) ** -0.25
IN_SIZES = (
    MLA_HEADS * (QK_NOPE + QK_ROPE),
    KV_RANK,
    QK_ROPE,
    CONV_DIM,
    CONV_DIM,
    SSM_DIM,
    D_MODEL,
    D_MODEL,
    D_MODEL,
)
IN_OFFSETS = tuple(int(v) for v in np.cumsum((0,) + IN_SIZES))
N_IN = IN_OFFSETS[-1]

kernel_name = "hybrid_mla_conformer_s5_deepnorm"


def _layer_norm(x, g, b, eps=1e-5):
    xf = x.astype(jnp.float32)
    mu = jnp.mean(xf, axis=-1, keepdims=True)
    var = jnp.mean(jnp.square(xf - mu), axis=-1, keepdims=True)
    y = (xf - mu) * lax.rsqrt(var + eps) * g.astype(jnp.float32) + b.astype(jnp.float32)
    return y.astype(x.dtype)


def _rms_norm(x, g, eps=1e-6):
    xf = x.astype(jnp.float32)
    y = xf * lax.rsqrt(jnp.mean(jnp.square(xf), axis=-1, keepdims=True) + eps) * g.astype(jnp.float32)
    return y.astype(x.dtype)


def _rope_tables(length):
    half = QK_ROPE // 2
    inv = jnp.power(ROPE_BASE, -jnp.arange(half, dtype=jnp.float32) / half)
    ang = jnp.arange(length, dtype=jnp.float32)[:, None] * inv[None, :]
    return jnp.cos(ang), jnp.sin(ang)


def _apply_rope(x, cos, sin):
    xf = x.astype(jnp.float32)
    x1, x2 = jnp.split(xf, 2, axis=-1)
    return jnp.concatenate([x1 * cos - x2 * sin, x1 * sin + x2 * cos], axis=-1).astype(x.dtype)


def _chunk_ids(length):
    pos = jnp.arange(length)
    return jnp.where(pos < N_META, 0, (pos - N_META) // CHUNK + 1)


def _mla_attend(qn, qr, q_cid, kn, kr, v, k_cid):
    s = jnp.einsum('bqhd,bkhd->bhqk', qn, kn) + jnp.einsum('bqhr,bkr->bhqk', qr, kr)
    s = s.astype(jnp.float32) * MLA_SCALE
    visible = k_cid[None, :] <= q_cid[:, None]
    s = jnp.where(visible[None, None], s, -jnp.inf)
    p = jax.nn.softmax(s, axis=-1).astype(v.dtype)
    return jnp.einsum('bhqk,bkhd->bqhd', p, v)


def _mla_branch(q, c_kv, k_rope, kv_norm_g, w_ukv, w_mla_out, cos, sin, cid):
    bsz, length, _ = q.shape
    q = q.reshape(bsz, length, MLA_HEADS, QK_NOPE + QK_ROPE)
    qn = q[..., :QK_NOPE]
    qr = _apply_rope(q[..., QK_NOPE:], cos[None, :, None, :], sin[None, :, None, :])
    kr = _apply_rope(k_rope, cos[None], sin[None])
    kv = (_rms_norm(c_kv, kv_norm_g) @ w_ukv).reshape(bsz, length, MLA_HEADS, QK_NOPE + V_DIM)
    kn, v = kv[..., :QK_NOPE], kv[..., QK_NOPE:]
    o_meta = _mla_attend(qn[:, :N_META], qr[:, :N_META], cid[:N_META],
                         kn[:, :N_META], kr[:, :N_META], v[:, :N_META], cid[:N_META])
    n_blk = (length - N_META) // Q_BLOCK

    def query_block(i):
        start = N_META + i * Q_BLOCK
        qn_b = lax.dynamic_slice_in_dim(qn, start, Q_BLOCK, axis=1)
        qr_b = lax.dynamic_slice_in_dim(qr, start, Q_BLOCK, axis=1)
        cid_b = lax.dynamic_slice_in_dim(cid, start, Q_BLOCK, axis=0)
        return _mla_attend(qn_b, qr_b, cid_b, kn, kr, v, cid)

    o_real = lax.map(query_block, jnp.arange(n_blk))
    o_real = jnp.moveaxis(o_real, 0, 1).reshape(bsz, length - N_META, MLA_HEADS * V_DIM)
    o = jnp.concatenate([o_meta.reshape(bsz, N_META, MLA_HEADS * V_DIM), o_real], axis=1)
    return o @ w_mla_out


def _conv_branch(val, gate, conv_w, conv_b, ln_g, ln_b, w_conv_out):
    z = val * jax.nn.sigmoid(gate)
    z = lax.conv_general_dilated(z, conv_w[:, None, :], window_strides=(1,),
                                 padding=[(CONV_K - 1, 0)],
                                 dimension_numbers=('NWC', 'WIO', 'NWC'),
                                 feature_group_count=CONV_DIM) + conv_b
    z = jax.nn.silu(_layer_norm(z, ln_g, ln_b))
    return z @ w_conv_out


def _ssm_combine(e1, e2):
    a1r, a1i, b1r, b1i = e1
    a2r, a2i, b2r, b2i = e2
    return (a2r * a1r - a2i * a1i, a2r * a1i + a2i * a1r,
            a2r * b1r - a2i * b1i + b2r, a2r * b1i + a2i * b1r + b2i)


def _ssm_segment(h_re, h_im, u_seg, a_re, a_im, bb_re, bb_im, c_re, c_im):
    d_re = jnp.einsum('btgc,gpc->btgp', u_seg, bb_re)
    d_im = jnp.einsum('btgc,gpc->btgp', u_seg, bb_im)
    ar = jnp.broadcast_to(a_re, d_re.shape)
    ai = jnp.broadcast_to(a_im, d_re.shape)
    cum_re, cum_im, x_re, x_im = lax.associative_scan(_ssm_combine, (ar, ai, d_re, d_im), axis=1)
    x_re2 = x_re + cum_re * h_re[:, None] - cum_im * h_im[:, None]
    x_im2 = x_im + cum_re * h_im[:, None] + cum_im * h_re[:, None]
    y = jnp.einsum('btgp,gcp->btgc', x_re2, c_re) - jnp.einsum('btgp,gcp->btgc', x_im2, c_im)
    return x_re2[:, -1], x_im2[:, -1], y


def _ssm_branch(u, lam_re, lam_im, log_dt, b_re, b_im, c_re, c_im, d_skip, w_glu):
    f32 = jnp.float32
    bsz, length, _ = u.shape
    lam_re = lam_re.astype(f32)
    lam_im = lam_im.astype(f32)
    dt = jnp.exp(log_dt.astype(f32))[:, None]
    mag = jnp.exp(lam_re * dt)
    ang = lam_im * dt
    a_re, a_im = mag * jnp.cos(ang), mag * jnp.sin(ang)
    den = jnp.square(lam_re) + jnp.square(lam_im)
    f_re = ((a_re - 1.0) * lam_re + a_im * lam_im) / den
    f_im = (a_im * lam_re - (a_re - 1.0) * lam_im) / den
    br, bi = b_re.astype(f32), b_im.astype(f32)
    bb_re = f_re[..., None] * br - f_im[..., None] * bi
    bb_im = f_re[..., None] * bi + f_im[..., None] * br
    cr, ci = c_re.astype(f32), c_im.astype(f32)
    uf = u.astype(f32)
    ug = uf.reshape(bsz, length, SSM_GROUPS, SSM_GROUP)
    h0 = jnp.zeros((bsz, SSM_GROUPS, SSM_STATE), f32)
    h_re, h_im, y_meta = _ssm_segment(h0, h0, ug[:, :N_META], a_re, a_im, bb_re, bb_im, cr, ci)
    n_chunk = (length - N_META) // CHUNK
    u_chunks = jnp.moveaxis(ug[:, N_META:].reshape(bsz, n_chunk, CHUNK, SSM_GROUPS, SSM_GROUP), 1, 0)

    def step(carry, uc):
        hr, hi, yc = _ssm_segment(carry[0], carry[1], uc, a_re, a_im, bb_re, bb_im, cr, ci)
        return (hr, hi), yc

    _, y_real = lax.scan(step, (h_re, h_im), u_chunks)
    y_real = jnp.moveaxis(y_real, 0, 1).reshape(bsz, length - N_META, SSM_DIM)
    y = jnp.concatenate([y_meta.reshape(bsz, N_META, SSM_DIM), y_real], axis=1)
    y = y + d_skip.astype(f32) * uf
    z = jax.nn.gelu(y).astype(u.dtype) @ w_glu
    za, zb = jnp.split(z, 2, axis=-1)
    return za * jax.nn.sigmoid(zb)


def setup_inputs(seed: int = 0) -> dict:
    key = jax.random.key(seed)
    keys = jax.random.split(key, 40)
    counter = [0]
    f32 = jnp.float32

    def nk():
        k = keys[counter[0]]
        counter[0] += 1
        return k

    def nrm(shape, scale):
        return jax.random.normal(nk(), shape, f32) * scale

    beta = DEEPNORM_BETA
    L_ = DEPTH
    x = nrm((BATCH, SEQ, D_MODEL), 1.0)
    meta_tokens = nrm((N_META, D_MODEL), 1.0)
    ln0_g = 1.0 + nrm((D_MODEL,), 0.02)
    ln0_b = nrm((D_MODEL,), 0.02)
    w_in = nrm((L_, D_MODEL, N_IN), D_MODEL ** -0.5)
    b_gate = nrm((L_, 3 * D_MODEL), 0.01)
    kv_norm_g = 1.0 + nrm((L_, KV_RANK), 0.02)
    w_ukv = nrm((L_, KV_RANK, MLA_HEADS * (QK_NOPE + V_DIM)), KV_RANK ** -0.5)
    w_mla_out = nrm((L_, MLA_HEADS * V_DIM, D_MODEL), beta * (MLA_HEADS * V_DIM) ** -0.5)
    conv_w = nrm((L_, CONV_K, CONV_DIM), CONV_K ** -0.5)
    conv_b = nrm((L_, CONV_DIM), 0.01)
    conv_ln_g = 1.0 + nrm((L_, CONV_DIM), 0.02)
    conv_ln_b = nrm((L_, CONV_DIM), 0.02)
    w_conv_out = nrm((L_, CONV_DIM, D_MODEL), beta * CONV_DIM ** -0.5)
    n_idx = jnp.arange(SSM_STATE, dtype=f32)
    ssm_lam_re = -0.5 + nrm((L_, SSM_GROUPS, SSM_STATE), 0.01)
    ssm_lam_im = math.pi * n_idx + nrm((L_, SSM_GROUPS, SSM_STATE), 0.01)
    ssm_log_dt = jax.random.uniform(nk(), (L_, SSM_GROUPS), f32, math.log(1e-3), math.log(1e-1))
    ssm_b_re = nrm((L_, SSM_GROUPS, SSM_STATE, SSM_GROUP), (2 * SSM_GROUP) ** -0.5)
    ssm_b_im = nrm((L_, SSM_GROUPS, SSM_STATE, SSM_GROUP), (2 * SSM_GROUP) ** -0.5)
    ssm_c_re = nrm((L_, SSM_GROUPS, SSM_GROUP, SSM_STATE), (2 * SSM_STATE) ** -0.5)
    ssm_c_im = nrm((L_, SSM_GROUPS, SSM_GROUP, SSM_STATE), (2 * SSM_STATE) ** -0.5)
    ssm_d = nrm((L_, SSM_DIM), 1.0)
    w_glu = nrm((L_, SSM_DIM, 2 * D_MODEL), beta * SSM_DIM ** -0.5)
    w_out = nrm((L_, D_MODEL, D_MODEL), beta * D_MODEL ** -0.5)
    ln1_g = 1.0 + nrm((L_, D_MODEL), 0.02)
    ln1_b = nrm((L_, D_MODEL), 0.02)
    w_ffn_gate = nrm((L_, D_MODEL, FFN_DIM), D_MODEL ** -0.5)
    w_ffn_up = nrm((L_, D_MODEL, FFN_DIM), beta * D_MODEL ** -0.5)
    w_ffn_down = nrm((L_, FFN_DIM, D_MODEL), beta * FFN_DIM ** -0.5)
    ln2_g = 1.0 + nrm((L_, D_MODEL), 0.02)
    ln2_b = nrm((L_, D_MODEL), 0.02)
    return {
        'x': x, 'meta_tokens': meta_tokens, 'ln0_g': ln0_g, 'ln0_b': ln0_b,
        'w_in': w_in, 'b_gate': b_gate, 'kv_norm_g': kv_norm_g, 'w_ukv': w_ukv,
        'w_mla_out': w_mla_out, 'conv_w': conv_w, 'conv_b': conv_b,
        'conv_ln_g': conv_ln_g, 'conv_ln_b': conv_ln_b, 'w_conv_out': w_conv_out,
        'ssm_lam_re': ssm_lam_re, 'ssm_lam_im': ssm_lam_im, 'ssm_log_dt': ssm_log_dt,
        'ssm_b_re': ssm_b_re, 'ssm_b_im': ssm_b_im, 'ssm_c_re': ssm_c_re, 'ssm_c_im': ssm_c_im,
        'ssm_d': ssm_d, 'w_glu': w_glu, 'w_out': w_out, 'ln1_g': ln1_g, 'ln1_b': ln1_b,
        'w_ffn_gate': w_ffn_gate, 'w_ffn_up': w_ffn_up, 'w_ffn_down': w_ffn_down,
        'ln2_g': ln2_g, 'ln2_b': ln2_b,
    }


def reference(x, meta_tokens, ln0_g, ln0_b, w_in, b_gate, kv_norm_g, w_ukv, w_mla_out,
              conv_w, conv_b, conv_ln_g, conv_ln_b, w_conv_out, ssm_lam_re, ssm_lam_im,
              ssm_log_dt, ssm_b_re, ssm_b_im, ssm_c_re, ssm_c_im, ssm_d, w_glu, w_out,
              ln1_g, ln1_b, w_ffn_gate, w_ffn_up, w_ffn_down, ln2_g, ln2_b):
    bsz = x.shape[0]
    meta = jnp.broadcast_to(meta_tokens[None].astype(x.dtype), (bsz, N_META, x.shape[-1]))
    h = _layer_norm(jnp.concatenate([meta, x], axis=1), ln0_g, ln0_b)
    length = h.shape[1]
    cos, sin = _rope_tables(length)
    cid = _chunk_ids(length)
    o = IN_OFFSETS
    for l in range(DEPTH):
        p = h @ w_in[l]
        q = p[..., o[0]:o[1]]
        c_kv = p[..., o[1]:o[2]]
        k_rope = p[..., o[2]:o[3]]
        conv_val = p[..., o[3]:o[4]]
        conv_gate = p[..., o[4]:o[5]]
        u = p[..., o[5]:o[6]]
        gates = jax.nn.sigmoid(p[..., o[6]:o[9]] + b_gate[l])
        g_a = gates[..., :D_MODEL]
        g_b = gates[..., D_MODEL:2 * D_MODEL]
        g_c = gates[..., 2 * D_MODEL:]
        y_a = _mla_branch(q, c_kv, k_rope, kv_norm_g[l], w_ukv[l], w_mla_out[l], cos, sin, cid)
        y_b = _conv_branch(conv_val, conv_gate, conv_w[l], conv_b[l], conv_ln_g[l],
                           conv_ln_b[l], w_conv_out[l])
        y_c = _ssm_branch(u, ssm_lam_re[l], ssm_lam_im[l], ssm_log_dt[l], ssm_b_re[l],
                          ssm_b_im[l], ssm_c_re[l], ssm_c_im[l], ssm_d[l], w_glu[l])
        mixed = (g_a * y_a + g_b * y_b + g_c * y_c) @ w_out[l]
        h = _layer_norm(DEEPNORM_ALPHA * h + mixed, ln1_g[l], ln1_b[l])
        f = (jax.nn.silu(h @ w_ffn_gate[l]) * (h @ w_ffn_up[l])) @ w_ffn_down[l]
        h = _layer_norm(DEEPNORM_ALPHA * h + f, ln2_g[l], ln2_b[l])
    return h[:, N_META:]
```

```python
import functools
import math

import jax
import jax.numpy as jnp
from jax import lax
from jax.experimental import pallas as pl
from jax.experimental.pallas import tpu as pltpu

F32 = jnp.float32
BF16 = jnp.bfloat16

N_META = 16
CHUNK = 64
HEADS = 16
QK_NOPE = 128
QK_ROPE = 64
V_DIM = 128
KV_RANK = 512
ROPE_BASE = 10000.0
MLA_SCALE = (QK_NOPE + QK_ROPE) ** -0.5
CONV_K = 31
SSM_GROUP = 16
SSM_STATE = 64
LN_EPS = 1e-5
RMS_EPS = 1e-6

LANES = 128
HEAD_SLOT = 2 * LANES
CKV_COLS = KV_RANK + LANES
META_TILE = 512
CONV_HALO = 32
VMEM_LIMIT = 56 * 1024 * 1024

TM_MM = (1280, 1024, 768, 512)
TM_MIX = (640, 512)
TM_LN = 512
TM_KV = 512
TT_CONV = 256
BQ = 512


def _row_tile(rows, candidates):
    return next(t for t in candidates if rows % t == 0)


def _cparams(sem):
    return pltpu.CompilerParams(dimension_semantics=sem, vmem_limit_bytes=VMEM_LIMIT)


def _dot(a, b):
    return jnp.dot(a, b, preferred_element_type=F32)


def _dot_nt(a, b):
    return lax.dot_general(a, b, (((1,), (1,)), ((), ())), preferred_element_type=F32)


def _dot_exact(a, b):
    return jnp.dot(a, b, preferred_element_type=F32, precision=lax.Precision.HIGHEST)


def _layer_norm(x, g, b):
    mu = jnp.mean(x, axis=-1, keepdims=True)
    xc = x - mu
    var = jnp.mean(xc * xc, axis=-1, keepdims=True)
    return xc * lax.rsqrt(var + LN_EPS) * g + b


def _rope_slot(a, tbl):
    p = a * tbl
    return p + pltpu.roll(p, 64, 1)


def _ln0_kernel(x_ref, g_ref, b_ref, h_ref, hb_ref):
    y = _layer_norm(x_ref[...], g_ref[...], b_ref[...])
    h_ref[...] = y
    hb_ref[...] = y.astype(BF16)


def _ln0(x_rows, g, b):
    r, d = x_rows.shape
    tm = TM_LN
    return pl.pallas_call(
        _ln0_kernel,
        grid=(r // tm,),
        in_specs=[pl.BlockSpec((tm, d), lambda i: (i, 0)),
                  pl.BlockSpec((1, d), lambda i: (0, 0)),
                  pl.BlockSpec((1, d), lambda i: (0, 0))],
        out_specs=[pl.BlockSpec((tm, d), lambda i: (i, 0)),
                   pl.BlockSpec((tm, d), lambda i: (i, 0))],
        out_shape=[jax.ShapeDtypeStruct((r, d), F32), jax.ShapeDtypeStruct((r, d), BF16)],
        compiler_params=_cparams(("parallel",)),
        name="ln0",
    )(x_rows, g.reshape(1, d), b.reshape(1, d))


def _mm_kernel(x_ref, w_ref, o_ref):
    o_ref[...] = _dot(x_ref[...], w_ref[...]).astype(o_ref.dtype)


def _mm_sigmoid_kernel(x_ref, w_ref, b_ref, o_ref):
    o_ref[...] = jax.nn.sigmoid(_dot(x_ref[...], w_ref[...]) + b_ref[...]).astype(o_ref.dtype)


def _mm_qrope_kernel(x_ref, w_ref, t_ref, o_ref, *, heads_per_tile):
    acc = _dot(x_ref[...], w_ref[...])
    tbl = t_ref[...]
    for hh in range(heads_per_tile):
        c0 = hh * HEAD_SLOT
        o_ref[:, c0:c0 + LANES] = (acc[:, c0:c0 + LANES] * MLA_SCALE).astype(o_ref.dtype)
        o_ref[:, c0 + LANES:c0 + HEAD_SLOT] = _rope_slot(
            acc[:, c0 + LANES:c0 + HEAD_SLOT], tbl).astype(o_ref.dtype)


def _matmul(x, w, *, tn, bias=None, rope_tbl=None, name):
    r, k = x.shape
    n = w.shape[1]
    tm = _row_tile(r, TM_MM)
    in_specs = [pl.BlockSpec((tm, k), lambda j, i: (i, 0)),
                pl.BlockSpec((k, tn), lambda j, i: (0, j))]
    args = [x, w]
    if bias is not None:
        kern = _mm_sigmoid_kernel
        in_specs.append(pl.BlockSpec((1, tn), lambda j, i: (0, j)))
        args.append(bias.reshape(1, n))
    elif rope_tbl is not None:
        kern = functools.partial(_mm_qrope_kernel, heads_per_tile=tn // HEAD_SLOT)
        in_specs.append(pl.BlockSpec((tm, LANES), lambda j, i: (i, 0)))
        args.append(rope_tbl)
    else:
        kern = _mm_kernel
    return pl.pallas_call(
        kern,
        grid=(n // tn, r // tm),
        in_specs=in_specs,
        out_specs=pl.BlockSpec((tm, tn), lambda j, i: (i, j)),
        out_shape=jax.ShapeDtypeStruct((r, n), BF16),
        compiler_params=_cparams(("parallel", "parallel")),
        name=name,
    )(*args)


def _kv_up_kernel(c_ref, g_ref, w_ref, t_ref, k_ref, v_ref):
    c = c_ref[:, :KV_RANK].astype(F32)
    ms = jnp.mean(c * c, axis=-1, keepdims=True)
    cn = (c * lax.rsqrt(ms + RMS_EPS) * g_ref[...]).astype(BF16)
    kv = _dot(cn, w_ref[...])
    kr = _rope_slot(c_ref[:, KV_RANK:].astype(F32), t_ref[...])
    lane = lax.broadcasted_iota(jnp.int32, kr.shape, 1)
    kr = jnp.where(lane < QK_ROPE, kr, 0.0).astype(BF16)
    nk = HEADS * QK_NOPE
    v_ref[...] = kv[:, nk:].astype(BF16)
    for h in range(HEADS):
        k_ref[:, h * HEAD_SLOT:h * HEAD_SLOT + LANES] = kv[:, h * QK_NOPE:(h + 1) * QK_NOPE].astype(BF16)
        k_ref[:, h * HEAD_SLOT + LANES:(h + 1) * HEAD_SLOT] = kr


def _kv_up(ckv, g, w_ukv, k_tbl):
    r = ckv.shape[0]
    tm = TM_KV
    nk, nv = HEADS * HEAD_SLOT, HEADS * V_DIM
    return pl.pallas_call(
        _kv_up_kernel,
        grid=(r // tm,),
        in_specs=[pl.BlockSpec((tm, CKV_COLS), lambda i: (i, 0)),
                  pl.BlockSpec((1, KV_RANK), lambda i: (0, 0)),
                  pl.BlockSpec(w_ukv.shape, lambda i: (0, 0)),
                  pl.BlockSpec((tm, LANES), lambda i: (i, 0))],
        out_specs=[pl.BlockSpec((tm, nk), lambda i: (i, 0)),
                   pl.BlockSpec((tm, nv), lambda i: (i, 0))],
        out_shape=[jax.ShapeDtypeStruct((r, nk), BF16), jax.ShapeDtypeStruct((r, nv), BF16)],
        compiler_params=_cparams(("parallel",)),
        name="kv_up",
    )(ckv, g.reshape(1, KV_RANK), w_ukv, k_tbl)


def _meta_scores(q, km):
    s = _dot_nt(q, km)
    lane = lax.broadcasted_iota(jnp.int32, s.shape, 1)
    return jnp.where(lane < N_META, s, -jnp.inf)


def _attn_kernel(qi_ref, kj_ref, q_ref, k_ref, v_ref, km_ref, vm_ref, o_ref, m_ref, l_ref, acc_ref):
    step = pl.program_id(2)
    qi = qi_ref[step]
    kj = kj_ref[step]
    q = q_ref[...]

    @pl.when(kj == 0)
    def _():
        s = _meta_scores(q, km_ref[...])
        m = jnp.max(s, axis=-1, keepdims=True)
        p = jnp.exp(s - m)
        m_ref[...] = m
        l_ref[...] = jnp.sum(p, axis=-1, keepdims=True)
        acc_ref[...] = _dot(p.astype(BF16), vm_ref[...])

    def update(s):
        m_prev = m_ref[...]
        m_new = jnp.maximum(m_prev, jnp.max(s, axis=-1, keepdims=True))
        alpha = jnp.exp(m_prev - m_new)
        p = jnp.exp(s - m_new)
        l_ref[...] = alpha * l_ref[...] + jnp.sum(p, axis=-1, keepdims=True)
        acc_ref[...] = alpha * acc_ref[...] + _dot(p.astype(BF16), v_ref[...])
        m_ref[...] = m_new

    @pl.when(kj < qi)
    def _():
        update(_dot_nt(q, k_ref[...]))

    @pl.when(kj == qi)
    def _():
        s = _dot_nt(q, k_ref[...])
        rc = lax.broadcasted_iota(jnp.int32, s.shape, 0) // CHUNK
        cc = lax.broadcasted_iota(jnp.int32, s.shape, 1) // CHUNK
        update(jnp.where(cc <= rc, s, -jnp.inf))
        o_ref[...] = (acc_ref[...] / l_ref[...]).astype(o_ref.dtype)


def _attention(q, k, v, bsz, seq):
    r = q.shape[0]
    bq = BQ
    nq = seq // bq
    pairs = [(i, j) for i in range(nq) for j in range(i + 1)]
    qi = jnp.asarray([p[0] for p in pairs], jnp.int32)
    kj = jnp.asarray([p[1] for p in pairs], jnp.int32)
    meta_blk = (bsz * seq) // LANES
    grid_spec = pltpu.PrefetchScalarGridSpec(
        num_scalar_prefetch=2,
        grid=(bsz, HEADS, len(pairs)),
        in_specs=[
            pl.BlockSpec((bq, HEAD_SLOT), lambda b, h, s, qi, kj: (b * nq + qi[s], h)),
            pl.BlockSpec((bq, HEAD_SLOT), lambda b, h, s, qi, kj: (b * nq + kj[s], h)),
            pl.BlockSpec((bq, V_DIM), lambda b, h, s, qi, kj: (b * nq + kj[s], h)),
            pl.BlockSpec((LANES, HEAD_SLOT), lambda b, h, s, qi, kj: (meta_blk, h)),
            pl.BlockSpec((LANES, V_DIM), lambda b, h, s, qi, kj: (meta_blk, h)),
        ],
        out_specs=pl.BlockSpec((bq, V_DIM), lambda b, h, s, qi, kj: (b * nq + qi[s], h)),
        scratch_shapes=[pltpu.VMEM((bq, 1), F32), pltpu.VMEM((bq, 1), F32),
                        pltpu.VMEM((bq, V_DIM), F32)],
    )
    return pl.pallas_call(
        _attn_kernel,
        grid_spec=grid_spec,
        out_shape=jax.ShapeDtypeStruct((r, HEADS * V_DIM), BF16),
        compiler_params=_cparams(("parallel", "parallel", "arbitrary")),
        name="mla_attn",
    )(qi, kj, q, k, v, k, v)


def _attn_meta_kernel(q_ref, km_ref, vm_ref, o_in_ref, o_ref):
    del o_in_ref
    s = _meta_scores(q_ref[...], km_ref[...])
    m = jnp.max(s, axis=-1, keepdims=True)
    p = jnp.exp(s - m)
    l = jnp.sum(p, axis=-1, keepdims=True)
    o_ref[...] = (_dot(p.astype(BF16), vm_ref[...]) / l).astype(o_ref.dtype)


def _attention_meta(q, k, v, o, rows_real):
    meta_blk = rows_real // LANES
    tile_blk = rows_real // META_TILE
    return pl.pallas_call(
        _attn_meta_kernel,
        grid=(HEADS,),
        in_specs=[pl.BlockSpec((META_TILE, HEAD_SLOT), lambda h: (tile_blk, h)),
                  pl.BlockSpec((LANES, HEAD_SLOT), lambda h: (meta_blk, h)),
                  pl.BlockSpec((LANES, V_DIM), lambda h: (meta_blk, h)),
                  pl.BlockSpec(memory_space=pl.ANY)],
        out_specs=pl.BlockSpec((META_TILE, V_DIM), lambda h: (tile_blk, h)),
        out_shape=jax.ShapeDtypeStruct(o.shape, o.dtype),
        input_output_aliases={3: 0},
        compiler_params=_cparams(("arbitrary",)),
        name="mla_attn_meta",
    )(q, k, v, o)


CONV_RB = 32
CONV_LB = 512


def _conv_kernel(val_ref, gate_ref, hval_ref, hgate_ref, mval_ref, mgate_ref, w_ref, cb_ref,
                 g_ref, b_ref, o_ref, z_ref, c_ref, *, tt, tiles_per_seq, n_real_tiles):
    i = pl.program_id(0)
    chans = o_ref.shape[1]

    def glu(v, g):
        return v.astype(F32) * jax.nn.sigmoid(g.astype(F32))

    z_ref[CONV_HALO:, :] = glu(val_ref[...], gate_ref[...])
    is_meta = i >= n_real_tiles
    first = jnp.logical_and(i % tiles_per_seq == 0, jnp.logical_not(is_meta))
    z_prev = glu(hval_ref[...], hgate_ref[...])
    z_meta = glu(mval_ref[...], mgate_ref[...])
    pad = CONV_HALO - N_META
    z_start = jnp.concatenate([jnp.zeros((pad, chans), F32), z_meta[:N_META]], axis=0)
    z_ref[:CONV_HALO, :] = jnp.where(is_meta, 0.0, jnp.where(first, z_start, z_prev))

    off = CONV_HALO - (CONV_K - 1)
    for rb in range(tt // CONV_RB):
        for lb in range(chans // CONV_LB):
            cs = slice(lb * CONV_LB, (lb + 1) * CONV_LB)
            acc = jnp.zeros((CONV_RB, CONV_LB), F32)
            for k in range(CONV_K):
                r0 = rb * CONV_RB + off + k
                acc = acc + z_ref[r0:r0 + CONV_RB, cs] * w_ref[k:k + 1, cs]
            c_ref[rb * CONV_RB:(rb + 1) * CONV_RB, cs] = acc + cb_ref[:, cs]
    y = _layer_norm(c_ref[...], g_ref[...], b_ref[...])
    o_ref[...] = (y * jax.nn.sigmoid(y)).astype(o_ref.dtype)


def _conv_branch(pc, conv_w, conv_b, ln_g, ln_b, rows_real, seq):
    r = pc.shape[0]
    chans = conv_w.shape[1]
    tt = TT_CONV
    hb = tt // CONV_HALO
    kern = functools.partial(_conv_kernel, tt=tt, tiles_per_seq=seq // tt,
                             n_real_tiles=rows_real // tt)
    meta_blk = rows_real // CONV_HALO

    def halo(i):
        return jnp.maximum(i * hb - 1, 0)

    row = lambda a: a.reshape(1, chans)
    return pl.pallas_call(
        kern,
        grid=(r // tt,),
        in_specs=[pl.BlockSpec((tt, chans), lambda i: (i, 0)),
                  pl.BlockSpec((tt, chans), lambda i: (i, 1)),
                  pl.BlockSpec((CONV_HALO, chans), lambda i: (halo(i), 0)),
                  pl.BlockSpec((CONV_HALO, chans), lambda i: (halo(i), 1)),
                  pl.BlockSpec((CONV_HALO, chans), lambda i: (meta_blk, 0)),
                  pl.BlockSpec((CONV_HALO, chans), lambda i: (meta_blk, 1)),
                  pl.BlockSpec((CONV_K, chans), lambda i: (0, 0)),
                  pl.BlockSpec((1, chans), lambda i: (0, 0)),
                  pl.BlockSpec((1, chans), lambda i: (0, 0)),
                  pl.BlockSpec((1, chans), lambda i: (0, 0))],
        out_specs=pl.BlockSpec((tt, chans), lambda i: (i, 0)),
        out_shape=jax.ShapeDtypeStruct((r, chans), BF16),
        scratch_shapes=[pltpu.VMEM((CONV_HALO + tt, chans), F32), pltpu.VMEM((tt, chans), F32)],
        compiler_params=_cparams(("parallel",)),
        name="conv_branch",
    )(pc, pc, pc, pc, pc, pc, conv_w, row(conv_b), row(ln_g), row(ln_b))


SSM_LANES = CHUNK * SSM_GROUP


def _ssm_prep_kernel(lc_ref, lr_ref, b_ref, bt_ref, ct_ref, e_ref, t_ref, pt_ref, qs_ref, al_ref):
    def discretise(lam_re, lam_im, log_dt):
        dt = jnp.exp(log_dt)
        mag = jnp.exp(lam_re * dt)
        ang = lam_im * dt
        a_re, a_im = mag * jnp.cos(ang), mag * jnp.sin(ang)
        den = lam_re * lam_re + lam_im * lam_im
        f_re = ((a_re - 1.0) * lam_re + a_im * lam_im) / den
        f_im = (a_im * lam_re - (a_re - 1.0) * lam_im) / den
        return lam_re * dt, ang, a_re, a_im, f_re, f_im

    def power(lre_dt, ang, e):
        mag = jnp.exp(lre_dt * e)
        return mag * jnp.cos(ang * e), mag * jnp.sin(ang * e)

    lc = lc_ref[0]
    lre_dt, ang, a_re, a_im, f_re, f_im = discretise(lc[:, 0:1], lc[:, 1:2], lc[:, 2:3])
    lag = (lax.broadcasted_iota(jnp.int32, (1, SSM_LANES), 1) // SSM_GROUP).astype(F32)
    al_re, al_im = power(lre_dt, ang, lag)
    ae_re, ae_im = power(lre_dt, ang, (CHUNK - 1.0) - lag)
    expand = e_ref[...]
    c_re = _dot_exact(ct_ref[0, 0], expand)
    c_im = _dot_exact(ct_ref[0, 1], expand)
    b_re = _dot_exact(b_ref[0, 0], expand)
    b_im = _dot_exact(b_ref[0, 1], expand)
    rf_re = al_re * c_re - al_im * c_im
    rf_im = al_re * c_im + al_im * c_re
    q_re = a_re * rf_re - a_im * rf_im
    q_im = a_re * rf_im + a_im * rf_re
    qs_ref[0, :SSM_STATE, :] = q_re.astype(qs_ref.dtype)
    qs_ref[0, SSM_STATE:, :] = (-q_im).astype(qs_ref.dtype)
    bb_re = f_re * b_re - f_im * b_im
    bb_im = f_re * b_im + f_im * b_re
    pt_ref[0, :SSM_STATE, :] = (ae_re * bb_re - ae_im * bb_im).astype(pt_ref.dtype)
    pt_ref[0, SSM_STATE:, :] = (ae_re * bb_im + ae_im * bb_re).astype(pt_ref.dtype)

    lr = lr_ref[0]
    lre_dt_r, ang_r, _, _, fr_re, fr_im = discretise(lr[0:1], lr[1:2], lr[2:3])
    ac_re, ac_im = power(lre_dt_r, ang_r, float(CHUNK))
    lane = lax.broadcasted_iota(jnp.int32, (1, LANES), 1)
    al_ref[0, 0:1, :] = ac_re
    al_ref[0, 1:2, :] = jnp.where(lane < SSM_STATE, -ac_im, ac_im)
    bt_re, bt_im = bt_ref[0, 0], bt_ref[0, 1]
    fr_re, fr_im = fr_re[:, :SSM_STATE], fr_im[:, :SSM_STATE]
    bbt_re = fr_re * bt_re - fr_im * bt_im
    bbt_im = fr_re * bt_im + fr_im * bt_re
    strip = _dot_exact(bbt_re, rf_re) - _dot_exact(bbt_im, rf_im)
    lane_t = lax.broadcasted_iota(jnp.int32, strip.shape, 1)
    for s in range(CHUNK):
        rolled = strip if s == 0 else pltpu.roll(strip, s * SSM_GROUP, 1)
        t_ref[0, s * SSM_GROUP:(s + 1) * SSM_GROUP, :] = jnp.where(
            lane_t >= s * SSM_GROUP, rolled, 0.0).astype(t_ref.dtype)


def _ssm_prep(lam_re, lam_im, log_dt, b_re, b_im, c_re, c_im):
    nl, ng, ns = lam_re.shape
    lg = nl * ng
    dtb = jnp.broadcast_to(log_dt[..., None], lam_re.shape)
    lcol = jnp.stack([lam_re, lam_im, dtb], axis=-1).reshape(lg, ns, 3)
    lrow = jnp.stack([lam_re, lam_im, dtb], axis=-2)
    lrow = jnp.concatenate([lrow, lrow], axis=-1).reshape(lg, 3, 2 * ns)
    b = jnp.stack([b_re, b_im], axis=2).reshape(lg, 2, ns, SSM_GROUP)
    bt = jnp.swapaxes(b, -1, -2)
    ct = jnp.swapaxes(jnp.stack([c_re, c_im], axis=2).reshape(lg, 2, SSM_GROUP, ns), -1, -2)
    lane_c = jnp.arange(SSM_LANES) % SSM_GROUP
    expand = (lane_c[None, :] == jnp.arange(SSM_GROUP)[:, None]).astype(F32)
    blk = lambda *shape: pl.BlockSpec((1,) + shape, lambda i: (i,) + (0,) * len(shape))
    return pl.pallas_call(
        _ssm_prep_kernel,
        grid=(lg,),
        in_specs=[blk(ns, 3), blk(3, 2 * ns), blk(2, ns, SSM_GROUP), blk(2, SSM_GROUP, ns),
                  blk(2, ns, SSM_GROUP), pl.BlockSpec((SSM_GROUP, SSM_LANES), lambda i: (0, 0))],
        out_specs=[blk(SSM_LANES, SSM_LANES), blk(2 * ns, SSM_LANES), blk(2 * ns, SSM_LANES),
                   blk(2, 2 * ns)],
        out_shape=[jax.ShapeDtypeStruct((lg, SSM_LANES, SSM_LANES), BF16),
                   jax.ShapeDtypeStruct((lg, 2 * ns, SSM_LANES), BF16),
                   jax.ShapeDtypeStruct((lg, 2 * ns, SSM_LANES), BF16),
                   jax.ShapeDtypeStruct((lg, 2, 2 * ns), F32)],
        compiler_params=_cparams(("parallel",)),
        name="ssm_prep",
    )(lcol, lrow, b, bt, ct, expand)


def _ssm_kernel(u_ref, t_ref, pt_ref, qs_ref, al_ref, d_ref, o_ref, s_ref, h_ref, *, nch, bsz):
    u = u_ref[0]
    rows = nch * bsz
    s_ref[:rows, :] = _dot_nt(u, pt_ref[0])
    s_ref[rows:, :] = jnp.zeros((s_ref.shape[0] - rows, LANES), F32)
    a1 = al_ref[0, 0:1, :]
    a2 = al_ref[0, 1:2, :]
    h = jnp.zeros((8, LANES), F32)
    for k in range(nch):
        h_ref[k * bsz:(k + 1) * bsz, :] = h[:bsz]
        h = h * a1 + pltpu.roll(h, SSM_STATE, 1) * a2 + s_ref[k * bsz:k * bsz + 8, :]
    y = _dot(u, t_ref[0]) + _dot(h_ref[...].astype(BF16), qs_ref[0])
    y = y + d_ref[0] * u.astype(F32)
    o_ref[0] = jax.nn.gelu(y, approximate=True).astype(o_ref.dtype)


def _ssm_branch(u_rows, prep, d_pat, layer, bsz, seq):
    t_all, pt_all, qs_all, al_all = prep
    chans = u_rows.shape[1]
    ng = chans // SSM_GROUP
    rows_real = bsz * seq
    nch = 1 + seq // CHUNK
    u_meta = u_rows[rows_real:rows_real + N_META]
    head = jnp.concatenate([jnp.zeros((CHUNK - N_META, chans), u_rows.dtype), u_meta], axis=0)
    useq = jnp.concatenate([jnp.broadcast_to(head[None], (bsz, CHUNK, chans)),
                            u_rows[:rows_real].reshape(bsz, seq, chans)], axis=1)
    u2 = useq.reshape(bsz, nch, CHUNK, ng, SSM_GROUP).transpose(3, 1, 0, 2, 4)
    u2 = u2.reshape(ng, nch * bsz, SSM_LANES)
    rows = nch * bsz
    assert bsz <= 8
    g0 = layer * ng
    kern = functools.partial(_ssm_kernel, nch=nch, bsz=bsz)
    y2 = pl.pallas_call(
        kern,
        grid=(ng,),
        in_specs=[pl.BlockSpec((1, rows, SSM_LANES), lambda g: (g, 0, 0)),
                  pl.BlockSpec((1, SSM_LANES, SSM_LANES), lambda g: (g0 + g, 0, 0)),
                  pl.BlockSpec((1, LANES, SSM_LANES), lambda g: (g0 + g, 0, 0)),
                  pl.BlockSpec((1, LANES, SSM_LANES), lambda g: (g0 + g, 0, 0)),
                  pl.BlockSpec((1, 2, LANES), lambda g: (g0 + g, 0, 0)),
                  pl.BlockSpec((1, 1, SSM_LANES), lambda g: (g0 + g, 0, 0))],
        out_specs=pl.BlockSpec((1, rows, SSM_LANES), lambda g: (g, 0, 0)),
        out_shape=jax.ShapeDtypeStruct((ng, rows, SSM_LANES), BF16),
        scratch_shapes=[pltpu.VMEM((rows + 8, LANES), F32), pltpu.VMEM((rows, LANES), F32)],
        compiler_params=_cparams(("parallel",)),
        name="ssm_scan",
    )(u2, t_all, pt_all, qs_all, al_all, d_pat)
    yseq = y2.reshape(ng, nch, bsz, CHUNK, SSM_GROUP).transpose(2, 1, 3, 0, 4)
    yseq = yseq.reshape(bsz, nch * CHUNK, chans)
    return jnp.concatenate([yseq[:, CHUNK:].reshape(rows_real, chans),
                            yseq[0, CHUNK - N_META:CHUNK],
                            jnp.zeros((META_TILE - N_META, chans), yseq.dtype)], axis=0)


def _mix_kernel(o_ref, zc_ref, gy_ref, ga_ref, gb_ref, gc_ref, wa_ref, wb_ref, wza_ref, wzb_ref,
                out_ref):
    ya = _dot(o_ref[...], wa_ref[...])
    yb = _dot(zc_ref[...], wb_ref[...])
    gy = gy_ref[...]
    yc = _dot(gy, wza_ref[...]) * jax.nn.sigmoid(_dot(gy, wzb_ref[...]))
    mix = (ga_ref[...].astype(F32) * ya + gb_ref[...].astype(F32) * yb
           + gc_ref[...].astype(F32) * yc)
    out_ref[...] = mix.astype(out_ref.dtype)


def _mix(o, zc, gy, gates, w_mla_out, w_conv_out, w_glu):
    r, d = o.shape
    tm, tn = _row_tile(r, TM_MIX), 512
    nb = d // tn
    act = lambda: pl.BlockSpec((tm, d), lambda j, i: (i, 0))
    gate = lambda k: pl.BlockSpec((tm, tn), lambda j, i: (i, k * nb + j))
    wgt = lambda k: pl.BlockSpec((d, tn), lambda j, i: (0, k * nb + j))
    return pl.pallas_call(
        _mix_kernel,
        grid=(nb, r // tm),
        in_specs=[act(), act(), act(), gate(0), gate(1), gate(2), wgt(0), wgt(0), wgt(0), wgt(1)],
        out_specs=pl.BlockSpec((tm, tn), lambda j, i: (i, j)),
        out_shape=jax.ShapeDtypeStruct((r, d), BF16),
        compiler_params=_cparams(("parallel", "parallel")),
        name="gated_mix",
    )(o, zc, gy, gates, gates, gates, w_mla_out, w_conv_out, w_glu, w_glu)


def _mm_res_ln_kernel(x_ref, w_ref, r_ref, g_ref, b_ref, h_ref, hb_ref, acc_ref, *, alpha, nk):
    kk = pl.program_id(1)

    @pl.when(kk == 0)
    def _():
        acc_ref[...] = jnp.zeros_like(acc_ref)

    acc_ref[...] += _dot(x_ref[...], w_ref[...])

    @pl.when(kk == nk - 1)
    def _():
        y = _layer_norm(alpha * r_ref[...] + acc_ref[...], g_ref[...], b_ref[...])
        h_ref[...] = y
        hb_ref[...] = y.astype(BF16)


def _mm_res_ln(x, w, res, g, b, *, alpha, tk):
    r, k = x.shape
    d = w.shape[1]
    tm = TM_LN
    nk = k // tk
    kern = functools.partial(_mm_res_ln_kernel, alpha=alpha, nk=nk)
    return pl.pallas_call(
        kern,
        grid=(r // tm, nk),
        in_specs=[pl.BlockSpec((tm, tk), lambda i, kk: (i, kk)),
                  pl.BlockSpec((tk, d), lambda i, kk: (kk, 0)),
                  pl.BlockSpec((tm, d), lambda i, kk: (i, 0)),
                  pl.BlockSpec((1, d), lambda i, kk: (0, 0)),
                  pl.BlockSpec((1, d), lambda i, kk: (0, 0))],
        out_specs=[pl.BlockSpec((tm, d), lambda i, kk: (i, 0)),
                   pl.BlockSpec((tm, d), lambda i, kk: (i, 0))],
        out_shape=[jax.ShapeDtypeStruct((r, d), F32), jax.ShapeDtypeStruct((r, d), BF16)],
        scratch_shapes=[pltpu.VMEM((tm, d), F32)],
        compiler_params=_cparams(("parallel", "arbitrary")),
        name="mm_res_ln",
    )(x, w, res, g.reshape(1, d), b.reshape(1, d))


def _ffn_up_kernel(x_ref, wg_ref, wu_ref, o_ref):
    x = x_ref[...]
    a = _dot(x, wg_ref[...])
    o_ref[...] = (a * jax.nn.sigmoid(a) * _dot(x, wu_ref[...])).astype(o_ref.dtype)


def _ffn_up(x, wg, wu):
    r, d = x.shape
    f = wg.shape[1]
    tm, tn = _row_tile(r, TM_MM), 512
    return pl.pallas_call(
        _ffn_up_kernel,
        grid=(f // tn, r // tm),
        in_specs=[pl.BlockSpec((tm, d), lambda j, i: (i, 0)),
                  pl.BlockSpec((d, tn), lambda j, i: (0, j)),
                  pl.BlockSpec((d, tn), lambda j, i: (0, j))],
        out_specs=pl.BlockSpec((tm, tn), lambda j, i: (i, j)),
        out_shape=jax.ShapeDtypeStruct((r, f), BF16),
        compiler_params=_cparams(("parallel", "parallel")),
        name="ffn_up",
    )(x, wg, wu)


def _rope_table(positions, scale):
    half = QK_ROPE // 2
    inv = jnp.power(ROPE_BASE, -jnp.arange(half, dtype=F32) / half)
    ang = positions.astype(F32)[:, None] * inv[None, :]
    cos, sin = jnp.cos(ang), jnp.sin(ang)
    return jnp.concatenate([cos, cos, -sin, sin], axis=-1) * scale


def _rope_cols(w):
    half = QK_ROPE // 2
    x1, x2 = w[:, :half], w[:, half:]
    return jnp.concatenate([x1, x2, x2, x1], axis=1)


def _split_w_in(w, d):
    qk = QK_NOPE + QK_ROPE
    o_q = HEADS * qk
    o_ckv = o_q + KV_RANK
    o_kr = o_ckv + QK_ROPE
    o_conv = o_kr + 2 * d
    o_u = o_conv + d
    wq = w[:, :o_q].reshape(d, HEADS, qk)
    slots = [jnp.concatenate([wq[:, h, :QK_NOPE], _rope_cols(wq[:, h, QK_NOPE:])], axis=1)
             for h in range(HEADS)]
    w_q = jnp.concatenate(slots, axis=1)
    w_ckv = jnp.concatenate([w[:, o_q:o_ckv], _rope_cols(w[:, o_ckv:o_kr])], axis=1)
    w_cu = w[:, o_kr:o_u]
    w_g = w[:, o_u:]
    return tuple(a.astype(BF16) for a in (w_q, w_ckv, w_cu, w_g))


def _split_w_ukv(w):
    wr = w.reshape(KV_RANK, HEADS, QK_NOPE + V_DIM)
    return jnp.concatenate([wr[:, :, :QK_NOPE].reshape(KV_RANK, HEADS * QK_NOPE),
                            wr[:, :, QK_NOPE:].reshape(KV_RANK, HEADS * V_DIM)], axis=1).astype(BF16)


def kernel(x, meta_tokens, ln0_g, ln0_b, w_in, b_gate, kv_norm_g, w_ukv, w_mla_out, conv_w, conv_b, conv_ln_g, conv_ln_b, w_conv_out, ssm_lam_re, ssm_lam_im, ssm_log_dt, ssm_b_re, ssm_b_im, ssm_c_re, ssm_c_im, ssm_d, w_glu, w_out, ln1_g, ln1_b, w_ffn_gate, w_ffn_up, w_ffn_down, ln2_g, ln2_b):
    bsz, seq, d = x.shape
    depth = w_in.shape[0]
    rows_real = bsz * seq
    assert seq % BQ == 0 and seq % TT_CONV == 0 and META_TILE % TM_LN == 0
    alpha = (2 * depth) ** 0.25

    x_rows = jnp.concatenate([x.reshape(rows_real, d), meta_tokens.astype(x.dtype),
                              jnp.zeros((META_TILE - N_META, d), x.dtype)], axis=0)
    pos = jnp.concatenate([jnp.tile(N_META + jnp.arange(seq), bsz), jnp.arange(META_TILE)])
    q_tbl = _rope_table(pos, MLA_SCALE)
    k_tbl = _rope_table(pos, 1.0)
    prep = _ssm_prep(ssm_lam_re, ssm_lam_im, ssm_log_dt, ssm_b_re, ssm_b_im, ssm_c_re, ssm_c_im)
    d_pat = jnp.tile(ssm_d.reshape(depth * (d // SSM_GROUP), 1, SSM_GROUP), (1, 1, CHUNK))

    h, hb = _ln0(x_rows, ln0_g, ln0_b)
    for l in range(depth):
        w_q, w_ckv, w_cu, w_g = _split_w_in(w_in[l], d)
        q = _matmul(hb, w_q, tn=4 * HEAD_SLOT, rope_tbl=q_tbl, name="in_proj_q")
        ckv = _matmul(hb, w_ckv, tn=CKV_COLS, name="in_proj_ckv")
        pc = _matmul(hb, w_cu, tn=1024, name="in_proj_conv_u")
        gates = _matmul(hb, w_g, tn=1024, bias=b_gate[l], name="in_proj_gates")

        k, v = _kv_up(ckv, kv_norm_g[l], _split_w_ukv(w_ukv[l]), k_tbl)
        o = _attention(q, k, v, bsz, seq)
        o = _attention_meta(q, k, v, o, rows_real)
        zc = _conv_branch(pc, conv_w[l], conv_b[l], conv_ln_g[l], conv_ln_b[l], rows_real, seq)
        gy = _ssm_branch(pc[:, 2 * d:], prep, d_pat, l, bsz, seq)

        mix = _mix(o, zc, gy, gates, w_mla_out[l].astype(BF16), w_conv_out[l].astype(BF16),
                   w_glu[l].astype(BF16))
        h, hb = _mm_res_ln(mix, w_out[l].astype(BF16), h, ln1_g[l], ln1_b[l], alpha=alpha, tk=d)
        ff = _ffn_up(hb, w_ffn_gate[l].astype(BF16), w_ffn_up[l].astype(BF16))
        h, hb = _mm_res_ln(ff, w_ffn_down[l].astype(BF16), h, ln2_g[l], ln2_b[l], alpha=alpha,
                           tk=w_ffn_down.shape[1] // 4)
    return h[:rows_real].reshape(bsz, seq, d)
```

```python
import functools
import math

import jax
import jax.numpy as jnp
from jax import lax
from jax.experimental import pallas as pl
from jax.experimental.pallas import tpu as pltpu

F32 = jnp.float32
BF16 = jnp.bfloat16

N_META = 16
CHUNK = 64
HEADS = 16
QK_NOPE = 128
QK_ROPE = 64
V_DIM = 128
KV_RANK = 512
ROPE_BASE = 10000.0
MLA_SCALE = (QK_NOPE + QK_ROPE) ** -0.5
LOG2E = math.log2(math.e)
Q_SCALE = MLA_SCALE * LOG2E
CONV_K = 31
SSM_GROUP = 16
SSM_STATE = 64
LN_EPS = 1e-5
RMS_EPS = 1e-6

LANES = 128
HEAD_SLOT = 2 * LANES
CKV_COLS = KV_RANK + LANES
META_TILE = 512
CONV_HALO = 32
VMEM_LIMIT = 56 * 1024 * 1024

TM_MM = (1280, 1024, 768, 512)
TM_MIX = (640, 512)
TM_LN = 512
TM_KV = 512
TT_CONV = 256
BQ = 1024
ATT_HEADS = 2
SSM_CHUNK = 32
SSM_GB = LANES // SSM_GROUP


def _row_tile(rows, candidates):
    return next(t for t in candidates if rows % t == 0)


def _cparams(sem):
    return pltpu.CompilerParams(dimension_semantics=sem, vmem_limit_bytes=VMEM_LIMIT)


def _dot(a, b):
    return jnp.dot(a, b, preferred_element_type=F32)


def _dot_nt(a, b):
    return lax.dot_general(a, b, (((1,), (1,)), ((), ())), preferred_element_type=F32)


def _dot_exact(a, b):
    return jnp.dot(a, b, preferred_element_type=F32, precision=lax.Precision.HIGHEST)


def _layer_norm(x, g, b):
    mu = jnp.mean(x, axis=-1, keepdims=True)
    xc = x - mu
    var = jnp.mean(xc * xc, axis=-1, keepdims=True)
    return xc * lax.rsqrt(var + LN_EPS) * g + b


def _rope_slot(a, tbl):
    p = a * tbl
    return p + pltpu.roll(p, 64, 1)


def _ln0_kernel(x_ref, g_ref, b_ref, h_ref, hb_ref):
    y = _layer_norm(x_ref[...], g_ref[...], b_ref[...])
    h_ref[...] = y
    hb_ref[...] = y.astype(BF16)


def _ln0(x_rows, g, b):
    r, d = x_rows.shape
    tm = TM_LN
    return pl.pallas_call(
        _ln0_kernel,
        grid=(r // tm,),
        in_specs=[pl.BlockSpec((tm, d), lambda i: (i, 0)),
                  pl.BlockSpec((1, d), lambda i: (0, 0)),
                  pl.BlockSpec((1, d), lambda i: (0, 0))],
        out_specs=[pl.BlockSpec((tm, d), lambda i: (i, 0)),
                   pl.BlockSpec((tm, d), lambda i: (i, 0))],
        out_shape=[jax.ShapeDtypeStruct((r, d), F32), jax.ShapeDtypeStruct((r, d), BF16)],
        compiler_params=_cparams(("parallel",)),
        name="ln0",
    )(x_rows, g.reshape(1, d), b.reshape(1, d))


def _mm_kernel(x_ref, w_ref, o_ref):
    o_ref[...] = _dot(x_ref[...], w_ref[...]).astype(o_ref.dtype)


def _mm_sigmoid_kernel(x_ref, w_ref, b_ref, o_ref):
    o_ref[...] = jax.nn.sigmoid(_dot(x_ref[...], w_ref[...]) + b_ref[...]).astype(o_ref.dtype)


def _mm_qrope_kernel(x_ref, w_ref, t_ref, o_ref, *, heads_per_tile):
    acc = _dot(x_ref[...], w_ref[...])
    tbl = t_ref[...]
    for hh in range(heads_per_tile):
        c0 = hh * HEAD_SLOT
        o_ref[:, c0:c0 + LANES] = (acc[:, c0:c0 + LANES] * Q_SCALE).astype(o_ref.dtype)
        o_ref[:, c0 + LANES:c0 + HEAD_SLOT] = _rope_slot(
            acc[:, c0 + LANES:c0 + HEAD_SLOT], tbl).astype(o_ref.dtype)


def _matmul(x, w, *, tn, bias=None, rope_tbl=None, out_dtype=BF16, name):
    r, k = x.shape
    n = w.shape[1]
    tm = _row_tile(r, TM_MM)
    in_specs = [pl.BlockSpec((tm, k), lambda j, i: (i, 0)),
                pl.BlockSpec((k, tn), lambda j, i: (0, j))]
    args = [x, w]
    if bias is not None:
        kern = _mm_sigmoid_kernel
        in_specs.append(pl.BlockSpec((1, tn), lambda j, i: (0, j)))
        args.append(bias.reshape(1, n))
    elif rope_tbl is not None:
        kern = functools.partial(_mm_qrope_kernel, heads_per_tile=tn // HEAD_SLOT)
        in_specs.append(pl.BlockSpec((tm, LANES), lambda j, i: (i, 0)))
        args.append(rope_tbl)
    else:
        kern = _mm_kernel
    return pl.pallas_call(
        kern,
        grid=(n // tn, r // tm),
        in_specs=in_specs,
        out_specs=pl.BlockSpec((tm, tn), lambda j, i: (i, j)),
        out_shape=jax.ShapeDtypeStruct((r, n), out_dtype),
        compiler_params=_cparams(("parallel", "parallel")),
        name=name,
    )(*args)


def _kv_up_kernel(c_ref, g_ref, wk_ref, wvt_ref, t_ref, k_ref, vt_ref):
    c = c_ref[:, :KV_RANK].astype(F32)
    ms = jnp.mean(c * c, axis=-1, keepdims=True)
    cn = (c * lax.rsqrt(ms + RMS_EPS) * g_ref[...]).astype(BF16)
    kn = _dot(cn, wk_ref[...])
    vt_ref[...] = _dot_nt(wvt_ref[...], cn).astype(BF16)
    kr = _rope_slot(c_ref[:, KV_RANK:].astype(F32), t_ref[...])
    lane = lax.broadcasted_iota(jnp.int32, kr.shape, 1)
    kr = jnp.where(lane < QK_ROPE, kr, 0.0).astype(BF16)
    for h in range(HEADS):
        k_ref[:, h * HEAD_SLOT:h * HEAD_SLOT + LANES] = kn[:, h * QK_NOPE:(h + 1) * QK_NOPE].astype(BF16)
        k_ref[:, h * HEAD_SLOT + LANES:(h + 1) * HEAD_SLOT] = kr


def _kv_up(ckv, g, w_k, w_vt, k_tbl):
    r = ckv.shape[0]
    tm = TM_KV
    nk, nv = HEADS * HEAD_SLOT, HEADS * V_DIM
    return pl.pallas_call(
        _kv_up_kernel,
        grid=(r // tm,),
        in_specs=[pl.BlockSpec((tm, CKV_COLS), lambda i: (i, 0)),
                  pl.BlockSpec((1, KV_RANK), lambda i: (0, 0)),
                  pl.BlockSpec(w_k.shape, lambda i: (0, 0)),
                  pl.BlockSpec(w_vt.shape, lambda i: (0, 0)),
                  pl.BlockSpec((tm, LANES), lambda i: (i, 0))],
        out_specs=[pl.BlockSpec((tm, nk), lambda i: (i, 0)),
                   pl.BlockSpec((nv, tm), lambda i: (0, i))],
        out_shape=[jax.ShapeDtypeStruct((r, nk), BF16), jax.ShapeDtypeStruct((nv, r), BF16)],
        compiler_params=_cparams(("parallel",)),
        name="kv_up",
    )(ckv, g.reshape(1, KV_RANK), w_k, w_vt, k_tbl)


def _attn_kernel(qi_ref, kj_ref, q_ref, k_ref, vt_ref, km_ref, vmt_ref, o_ref, m_ref, l_ref, acc_ref):
    step = pl.program_id(2)
    qi = qi_ref[step]
    kj = kj_ref[step]

    def head_q(hh):
        return q_ref[:, hh * HEAD_SLOT:(hh + 1) * HEAD_SLOT]

    def update(hh, st, vt):
        m_prev = m_ref[hh]
        m_new = jnp.maximum(m_prev, jnp.max(st, axis=0, keepdims=True))
        alpha = jnp.exp2(m_prev - m_new)
        p = jnp.exp2(st - m_new)
        l_ref[hh] = alpha * l_ref[hh] + jnp.sum(p, axis=0, keepdims=True)
        acc_ref[hh] = alpha * acc_ref[hh] + _dot(vt, p.astype(BF16))
        m_ref[hh] = m_new

    @pl.when(kj == 0)
    def _():
        for hh in range(ATT_HEADS):
            m_ref[hh] = jnp.full(m_ref.shape[1:], -jnp.inf, F32)
            l_ref[hh] = jnp.zeros(l_ref.shape[1:], F32)
            acc_ref[hh] = jnp.zeros(acc_ref.shape[1:], F32)
        for hh in range(ATT_HEADS):
            st = _dot_nt(km_ref[:, hh * HEAD_SLOT:(hh + 1) * HEAD_SLOT], head_q(hh))
            key = lax.broadcasted_iota(jnp.int32, st.shape, 0)
            update(hh, jnp.where(key < N_META, st, -jnp.inf),
                   vmt_ref[hh * V_DIM:(hh + 1) * V_DIM, :])

    def block(masked):
        for hh in range(ATT_HEADS):
            st = _dot_nt(k_ref[:, hh * HEAD_SLOT:(hh + 1) * HEAD_SLOT], head_q(hh))
            if masked:
                kc = lax.broadcasted_iota(jnp.int32, st.shape, 0) // CHUNK
                qc = lax.broadcasted_iota(jnp.int32, st.shape, 1) // CHUNK
                st = jnp.where(kc <= qc, st, -jnp.inf)
            update(hh, st, vt_ref[hh * V_DIM:(hh + 1) * V_DIM, :])

    @pl.when(kj < qi)
    def _():
        block(False)

    @pl.when(kj == qi)
    def _():
        block(True)
        for hh in range(ATT_HEADS):
            o = acc_ref[hh] / l_ref[hh]
            o_ref[:, hh * V_DIM:(hh + 1) * V_DIM] = o.T.astype(o_ref.dtype)


def _attention(q, k, vt, bsz, seq):
    r = q.shape[0]
    bq = BQ
    nq = seq // bq
    nh = ATT_HEADS
    pairs = [(i, j) for i in range(nq) for j in range(i + 1)]
    qi = jnp.asarray([p[0] for p in pairs], jnp.int32)
    kj = jnp.asarray([p[1] for p in pairs], jnp.int32)
    meta_blk = (bsz * seq) // LANES
    grid_spec = pltpu.PrefetchScalarGridSpec(
        num_scalar_prefetch=2,
        grid=(bsz, HEADS // nh, len(pairs)),
        in_specs=[
            pl.BlockSpec((bq, nh * HEAD_SLOT), lambda b, h, s, qi, kj: (b * nq + qi[s], h)),
            pl.BlockSpec((bq, nh * HEAD_SLOT), lambda b, h, s, qi, kj: (b * nq + kj[s], h)),
            pl.BlockSpec((nh * V_DIM, bq), lambda b, h, s, qi, kj: (h, b * nq + kj[s])),
            pl.BlockSpec((LANES, nh * HEAD_SLOT), lambda b, h, s, qi, kj: (meta_blk, h)),
            pl.BlockSpec((nh * V_DIM, LANES), lambda b, h, s, qi, kj: (h, meta_blk)),
        ],
        out_specs=pl.BlockSpec((bq, nh * V_DIM), lambda b, h, s, qi, kj: (b * nq + qi[s], h)),
        scratch_shapes=[pltpu.VMEM((nh, 1, bq), F32), pltpu.VMEM((nh, 1, bq), F32),
                        pltpu.VMEM((nh, V_DIM, bq), F32)],
    )
    return pl.pallas_call(
        _attn_kernel,
        grid_spec=grid_spec,
        out_shape=jax.ShapeDtypeStruct((r, HEADS * V_DIM), BF16),
        compiler_params=_cparams(("parallel", "parallel", "arbitrary")),
        name="mla_attn",
    )(qi, kj, q, k, vt, k, vt)


def _attn_meta_kernel(q_ref, km_ref, vmt_ref, o_in_ref, o_ref):
    del o_in_ref
    s = _dot_nt(q_ref[...], km_ref[...])
    lane = lax.broadcasted_iota(jnp.int32, s.shape, 1)
    s = jnp.where(lane < N_META, s, -jnp.inf)
    m = jnp.max(s, axis=-1, keepdims=True)
    p = jnp.exp2(s - m)
    l = jnp.sum(p, axis=-1, keepdims=True)
    o_ref[...] = (_dot_nt(p.astype(BF16), vmt_ref[...]) / l).astype(o_ref.dtype)


def _attention_meta(q, k, vt, o, rows_real):
    meta_blk = rows_real // LANES
    tile_blk = rows_real // META_TILE
    return pl.pallas_call(
        _attn_meta_kernel,
        grid=(HEADS,),
        in_specs=[pl.BlockSpec((META_TILE, HEAD_SLOT), lambda h: (tile_blk, h)),
                  pl.BlockSpec((LANES, HEAD_SLOT), lambda h: (meta_blk, h)),
                  pl.BlockSpec((V_DIM, LANES), lambda h: (h, meta_blk)),
                  pl.BlockSpec(memory_space=pl.ANY)],
        out_specs=pl.BlockSpec((META_TILE, V_DIM), lambda h: (tile_blk, h)),
        out_shape=jax.ShapeDtypeStruct(o.shape, o.dtype),
        input_output_aliases={3: 0},
        compiler_params=_cparams(("arbitrary",)),
        name="mla_attn_meta",
    )(q, k, vt, o)


CONV_RB = 32
CONV_LB = 512


def _conv_kernel(val_ref, gate_ref, hval_ref, hgate_ref, mval_ref, mgate_ref, w_ref, cb_ref,
                 g_ref, b_ref, o_ref, z_ref, c_ref, *, tt, tiles_per_seq, n_real_tiles):
    i = pl.program_id(0)
    chans = o_ref.shape[1]

    def glu(v, g):
        return v.astype(F32) * jax.nn.sigmoid(g.astype(F32))

    z_ref[CONV_HALO:, :] = glu(val_ref[...], gate_ref[...])
    is_meta = i >= n_real_tiles
    first = jnp.logical_and(i % tiles_per_seq == 0, jnp.logical_not(is_meta))
    z_prev = glu(hval_ref[...], hgate_ref[...])
    z_meta = glu(mval_ref[...], mgate_ref[...])
    pad = CONV_HALO - N_META
    z_start = jnp.concatenate([jnp.zeros((pad, chans), F32), z_meta[:N_META]], axis=0)
    z_ref[:CONV_HALO, :] = jnp.where(is_meta, 0.0, jnp.where(first, z_start, z_prev))

    off = CONV_HALO - (CONV_K - 1)
    for rb in range(tt // CONV_RB):
        for lb in range(chans // CONV_LB):
            cs = slice(lb * CONV_LB, (lb + 1) * CONV_LB)
            acc = jnp.zeros((CONV_RB, CONV_LB), F32)
            for k in range(CONV_K):
                r0 = rb * CONV_RB + off + k
                acc = acc + z_ref[r0:r0 + CONV_RB, cs] * w_ref[k:k + 1, cs]
            c_ref[rb * CONV_RB:(rb + 1) * CONV_RB, cs] = acc + cb_ref[:, cs]
    y = _layer_norm(c_ref[...], g_ref[...], b_ref[...])
    o_ref[...] = (y * jax.nn.sigmoid(y)).astype(o_ref.dtype)


def _conv_branch(pc, conv_w, conv_b, ln_g, ln_b, rows_real, seq):
    r = pc.shape[0]
    chans = conv_w.shape[1]
    tt = TT_CONV
    hb = tt // CONV_HALO
    kern = functools.partial(_conv_kernel, tt=tt, tiles_per_seq=seq // tt,
                             n_real_tiles=rows_real // tt)
    meta_blk = rows_real // CONV_HALO

    def halo(i):
        return jnp.maximum(i * hb - 1, 0)

    row = lambda a: a.reshape(1, chans)
    return pl.pallas_call(
        kern,
        grid=(r // tt,),
        in_specs=[pl.BlockSpec((tt, chans), lambda i: (i, 0)),
                  pl.BlockSpec((tt, chans), lambda i: (i, 1)),
                  pl.BlockSpec((CONV_HALO, chans), lambda i: (halo(i), 0)),
                  pl.BlockSpec((CONV_HALO, chans), lambda i: (halo(i), 1)),
                  pl.BlockSpec((CONV_HALO, chans), lambda i: (meta_blk, 0)),
                  pl.BlockSpec((CONV_HALO, chans), lambda i: (meta_blk, 1)),
                  pl.BlockSpec((CONV_K, chans), lambda i: (0, 0)),
                  pl.BlockSpec((1, chans), lambda i: (0, 0)),
                  pl.BlockSpec((1, chans), lambda i: (0, 0)),
                  pl.BlockSpec((1, chans), lambda i: (0, 0))],
        out_specs=pl.BlockSpec((tt, chans), lambda i: (i, 0)),
        out_shape=jax.ShapeDtypeStruct((r, chans), BF16),
        scratch_shapes=[pltpu.VMEM((CONV_HALO + tt, chans), F32), pltpu.VMEM((tt, chans), F32)],
        compiler_params=_cparams(("parallel",)),
        name="conv_branch",
    )(pc, pc, pc, pc, pc, pc, conv_w, row(conv_b), row(ln_g), row(ln_b))


SSM_SL = SSM_CHUNK * SSM_GROUP


def _ssm_prep_kernel(lc_ref, lr_ref, b_ref, c_ref, cc_ref, e_ref, et_ref, tt_ref, pt_ref, qst_ref,
                     sc_ref, *, n_steps):
    ns, ch = SSM_STATE, SSM_CHUNK

    def power(lre_dt, ang, e):
        mag = jnp.exp(lre_dt * e)
        return mag * jnp.cos(ang * e), mag * jnp.sin(ang * e)

    lc = lc_ref[0]
    lam_re, lam_im = lc[:, 0:1], lc[:, 1:2]
    dt = jnp.exp(lc[:, 2:3])
    lre_dt, ang = lam_re * dt, lam_im * dt
    a_re, a_im = power(lre_dt, ang, 1.0)
    den = lam_re * lam_re + lam_im * lam_im
    f_re = (((a_re - 1.0) * lam_re + a_im * lam_im) / den)[:ns]
    f_im = ((a_im * lam_re - (a_re - 1.0) * lam_im) / den)[:ns]
    step = (lax.broadcasted_iota(jnp.int32, (1, SSM_SL), 1) // SSM_GROUP).astype(F32)
    ae_re, ae_im = power(lre_dt[:ns], ang[:ns], (ch - 1.0) - step)
    expand = e_ref[...]
    b_re = _dot_exact(b_ref[0, 0], expand)
    b_im = _dot_exact(b_ref[0, 1], expand)
    bb_re = f_re * b_re - f_im * b_im
    bb_im = f_re * b_im + f_im * b_re
    p_re = ae_re * bb_re - ae_im * bb_im
    p_im = ae_re * bb_im + ae_im * bb_re
    pt_ref[0, :ns, :] = p_re.astype(pt_ref.dtype)
    pt_ref[0, ns:, :] = p_im.astype(pt_ref.dtype)
    rev = _dot_exact(c_ref[0, 0], p_re) - _dot_exact(c_ref[0, 1], p_im)
    lane_t = lax.broadcasted_iota(jnp.int32, rev.shape, 1)
    for t in range(ch):
        w = (t + 1) * SSM_GROUP
        rolled = rev if w == SSM_SL else pltpu.roll(rev, w, 1)
        tt_ref[0, t * SSM_GROUP:(t + 1) * SSM_GROUP, :] = jnp.where(
            lane_t < w, rolled, 0.0).astype(tt_ref.dtype)
    lane2 = lax.broadcasted_iota(jnp.int32, (LANES, LANES), 1)
    row2 = lax.broadcasted_iota(jnp.int32, (LANES, LANES), 0)
    expo = (ch * jnp.left_shift(1, jnp.minimum(lane2[0:1] // 2, n_steps))).astype(F32)
    s_re, s_im = power(lre_dt, ang, expo)
    sc_ref[0] = jnp.where(lane2 % 2 == 0, s_re, jnp.where(row2 < ns, -s_im, s_im))

    lr = lr_ref[0]
    dt_r = jnp.exp(lr[2:3])
    tp1 = (lax.broadcasted_iota(jnp.int32, (SSM_SL, 1), 0) // SSM_GROUP + 1).astype(F32)
    ap_re, ap_im = power(lr[0:1] * dt_r, lr[1:2] * dt_r, tp1)
    ctile = _dot_exact(et_ref[...], cc_ref[0])
    sign = jnp.where(lane2[0:1] < ns, 1.0, -1.0)
    qst_ref[0] = (sign * (ap_re * ctile) - pltpu.roll(ap_im * ctile, ns, 1)).astype(qst_ref.dtype)


def _ssm_prep(lam_re, lam_im, log_dt, b_re, b_im, c_re, c_im, n_steps):
    nl, ng, ns = lam_re.shape
    lg = nl * ng
    dtb = jnp.broadcast_to(log_dt[..., None], lam_re.shape)
    lrow = jnp.stack([lam_re, lam_im, dtb], axis=-2)
    lrow = jnp.concatenate([lrow, lrow], axis=-1).reshape(lg, 3, 2 * ns)
    lcol = jnp.swapaxes(lrow, -1, -2)
    b = jnp.stack([b_re, b_im], axis=2).reshape(lg, 2, ns, SSM_GROUP)
    c = jnp.stack([c_re, c_im], axis=2).reshape(lg, 2, SSM_GROUP, ns)
    ccat = jnp.concatenate([c_re, c_im], axis=-1).reshape(lg, SSM_GROUP, 2 * ns)
    lane_c = jnp.arange(SSM_SL) % SSM_GROUP
    expand = (lane_c[None, :] == jnp.arange(SSM_GROUP)[:, None]).astype(F32)
    blk = lambda *shape: pl.BlockSpec((1,) + shape, lambda i: (i,) + (0,) * len(shape))
    return pl.pallas_call(
        functools.partial(_ssm_prep_kernel, n_steps=n_steps),
        grid=(lg,),
        in_specs=[blk(2 * ns, 3), blk(3, 2 * ns), blk(2, ns, SSM_GROUP), blk(2, SSM_GROUP, ns),
                  blk(SSM_GROUP, 2 * ns),
                  pl.BlockSpec((SSM_GROUP, SSM_SL), lambda i: (0, 0)),
                  pl.BlockSpec((SSM_SL, SSM_GROUP), lambda i: (0, 0))],
        out_specs=[blk(SSM_SL, SSM_SL), blk(2 * ns, SSM_SL), blk(SSM_SL, 2 * ns), blk(2 * ns, 2 * ns)],
        out_shape=[jax.ShapeDtypeStruct((lg, SSM_SL, SSM_SL), BF16),
                   jax.ShapeDtypeStruct((lg, 2 * ns, SSM_SL), BF16),
                   jax.ShapeDtypeStruct((lg, SSM_SL, 2 * ns), BF16),
                   jax.ShapeDtypeStruct((lg, 2 * ns, 2 * ns), F32)],
        compiler_params=_cparams(("parallel",)),
        name="ssm_prep",
    )(lcol, lrow, b, c, ccat, expand, expand.T)


def _swap_halves(x):
    return pltpu.roll(x, SSM_STATE, 0)


def _ssm_kernel(u_ref, tt_ref, pt_ref, qst_ref, sc_ref, d_ref, um_ref, o_ref, ym_ref, ut_ref, yt_ref,
                *, nk, n_steps):
    ch = SSM_CHUNK
    for s in range(ch):
        ust = u_ref[pl.ds(s, nk, stride=ch), :].T
        for g in range(SSM_GB):
            ut_ref[g, s * SSM_GROUP:(s + 1) * SSM_GROUP, :] = (
                ust[g * SSM_GROUP:(g + 1) * SSM_GROUP, :].astype(BF16))
    lane = lax.broadcasted_iota(jnp.int32, (LANES, nk), 1)
    for g in range(SSM_GB):
        ut = ut_ref[g]
        um = um_ref[g]
        tt, pt, sc = tt_ref[g], pt_ref[g], sc_ref[g]
        ym_ref[g] = _dot_nt(um, tt)[:8]
        g0 = _dot_nt(pt, um)[:, 0:1]
        seed = sc[:, 0:1] * g0 + sc[:, 1:2] * _swap_halves(g0)
        e = _dot(pt, ut) + jnp.where(lane == 0, seed, 0.0)
        for i in range(n_steps):
            sh = jnp.where(lane >= 2 ** i, pltpu.roll(e, 2 ** i, 1), 0.0)
            e = e + sc[:, 2 * i:2 * i + 1] * sh + sc[:, 2 * i + 1:2 * i + 2] * _swap_halves(sh)
        state_in = jnp.where(lane == 0, g0, pltpu.roll(e, 1, 1))
        yt_ref[g] = _dot(tt, ut) + _dot(qst_ref[g], state_in.astype(BF16))
    for t in range(ch):
        yt = jnp.concatenate([yt_ref[g, t * SSM_GROUP:(t + 1) * SSM_GROUP, :]
                              for g in range(SSM_GB)], axis=0)
        y = yt.T + d_ref[...] * u_ref[pl.ds(t, nk, stride=ch), :]
        o_ref[pl.ds(t, nk, stride=ch), :] = jax.nn.gelu(y, approximate=True)


def _ssm_meta_kernel(ym_ref, um_ref, d_ref, o_in_ref, o_ref):
    del o_in_ref
    o_ref[...] = jnp.zeros(o_ref.shape, o_ref.dtype)
    y = ym_ref[...] + d_ref[...] * um_ref[...]
    o_ref[:N_META, :] = jax.nn.gelu(y, approximate=True)


def _ssm_branch(u_nat, prep, d_row, layer, bsz, seq):
    tt_all, pt_all, qst_all, sc_all = prep
    r, chans = u_nat.shape
    ng = chans // SSM_GROUP
    nj = ng // SSM_GB
    rows_real = bsz * seq
    ch = SSM_CHUNK
    nk = seq // ch
    n_steps = nk.bit_length() - 1
    assert nk == 2 ** n_steps and nk % LANES == 0 and ch >= N_META
    u_meta = u_nat[rows_real:rows_real + N_META]
    um = u_meta.reshape(N_META, ng, SSM_GROUP).transpose(1, 0, 2).reshape(ng, N_META * SSM_GROUP)
    um = jnp.pad(um, ((0, 0), ((ch - N_META) * SSM_GROUP, 0)))
    um = jnp.pad(um[:, None, :], ((0, 0), (0, LANES - 1), (0, 0))).astype(BF16)
    j0 = layer * nj
    blk = lambda *shape: pl.BlockSpec((SSM_GB,) + shape, lambda j, b: (j0 + j,) + (0,) * len(shape))
    kern = functools.partial(_ssm_kernel, nk=nk, n_steps=n_steps)
    gy, ym = pl.pallas_call(
        kern,
        grid=(nj, bsz),
        in_specs=[pl.BlockSpec((seq, LANES), lambda j, b: (b, j)),
                  blk(SSM_SL, SSM_SL), blk(LANES, SSM_SL), blk(SSM_SL, LANES), blk(LANES, LANES),
                  pl.BlockSpec((1, LANES), lambda j, b: (0, j)),
                  pl.BlockSpec((SSM_GB, LANES, SSM_SL), lambda j, b: (j, 0, 0))],
        out_specs=[pl.BlockSpec((seq, LANES), lambda j, b: (b, j)),
                   pl.BlockSpec((SSM_GB, 8, SSM_SL), lambda j, b: (j, 0, 0))],
        out_shape=[jax.ShapeDtypeStruct((r, chans), F32),
                   jax.ShapeDtypeStruct((ng, 8, SSM_SL), F32)],
        scratch_shapes=[pltpu.VMEM((SSM_GB, SSM_SL, nk), BF16), pltpu.VMEM((SSM_GB, SSM_SL, nk), F32)],
        compiler_params=_cparams(("parallel", "arbitrary")),
        name="ssm_scan",
    )(u_nat, tt_all, pt_all, qst_all, sc_all, d_row, um)
    y_meta = ym[:, 0, :].reshape(ng, ch, SSM_GROUP)[:, ch - N_META:, :]
    y_meta = y_meta.transpose(1, 0, 2).reshape(N_META, chans)
    tile_blk = rows_real // META_TILE
    return pl.pallas_call(
        _ssm_meta_kernel,
        grid=(1,),
        in_specs=[pl.BlockSpec((N_META, chans), lambda i: (0, 0)),
                  pl.BlockSpec((N_META, chans), lambda i: (0, 0)),
                  pl.BlockSpec((1, chans), lambda i: (0, 0)),
                  pl.BlockSpec(memory_space=pl.ANY)],
        out_specs=pl.BlockSpec((META_TILE, chans), lambda i: (tile_blk, 0)),
        out_shape=jax.ShapeDtypeStruct(gy.shape, gy.dtype),
        input_output_aliases={3: 0},
        compiler_params=_cparams(("arbitrary",)),
        name="ssm_meta",
    )(y_meta, u_meta, d_row, gy)


def _mix_kernel(o_ref, zc_ref, gy_ref, ga_ref, gb_ref, gc_ref, wa_ref, wb_ref, wza_ref, wzb_ref,
                out_ref):
    ya = _dot(o_ref[...], wa_ref[...])
    yb = _dot(zc_ref[...], wb_ref[...])
    gy = gy_ref[...].astype(BF16)
    yc = _dot(gy, wza_ref[...]) * jax.nn.sigmoid(_dot(gy, wzb_ref[...]))
    mix = (ga_ref[...].astype(F32) * ya + gb_ref[...].astype(F32) * yb
           + gc_ref[...].astype(F32) * yc)
    out_ref[...] = mix.astype(out_ref.dtype)


def _mix(o, zc, gy, gates, w_mla_out, w_conv_out, w_glu):
    r, d = o.shape
    tm, tn = _row_tile(r, TM_MIX), 512
    nb = d // tn
    act = lambda: pl.BlockSpec((tm, d), lambda j, i: (i, 0))
    gate = lambda k: pl.BlockSpec((tm, tn), lambda j, i: (i, k * nb + j))
    wgt = lambda k: pl.BlockSpec((d, tn), lambda j, i: (0, k * nb + j))
    return pl.pallas_call(
        _mix_kernel,
        grid=(nb, r // tm),
        in_specs=[act(), act(), act(), gate(0), gate(1), gate(2), wgt(0), wgt(0), wgt(0), wgt(1)],
        out_specs=pl.BlockSpec((tm, tn), lambda j, i: (i, j)),
        out_shape=jax.ShapeDtypeStruct((r, d), BF16),
        compiler_params=_cparams(("parallel", "parallel")),
        name="gated_mix",
    )(o, zc, gy, gates, gates, gates, w_mla_out, w_conv_out, w_glu, w_glu)


def _mm_res_ln_kernel(x_ref, w_ref, r_ref, g_ref, b_ref, h_ref, hb_ref, acc_ref, *, alpha, nk):
    kk = pl.program_id(1)

    @pl.when(kk == 0)
    def _():
        acc_ref[...] = jnp.zeros_like(acc_ref)

    acc_ref[...] += _dot(x_ref[...], w_ref[...])

    @pl.when(kk == nk - 1)
    def _():
        y = _layer_norm(alpha * r_ref[...] + acc_ref[...], g_ref[...], b_ref[...])
        h_ref[...] = y
        hb_ref[...] = y.astype(BF16)


def _mm_res_ln(x, w, res, g, b, *, alpha, tk):
    r, k = x.shape
    d = w.shape[1]
    tm = TM_LN
    nk = k // tk
    kern = functools.partial(_mm_res_ln_kernel, alpha=alpha, nk=nk)
    return pl.pallas_call(
        kern,
        grid=(r // tm, nk),
        in_specs=[pl.BlockSpec((tm, tk), lambda i, kk: (i, kk)),
                  pl.BlockSpec((tk, d), lambda i, kk: (kk, 0)),
                  pl.BlockSpec((tm, d), lambda i, kk: (i, 0)),
                  pl.BlockSpec((1, d), lambda i, kk: (0, 0)),
                  pl.BlockSpec((1, d), lambda i, kk: (0, 0))],
        out_specs=[pl.BlockSpec((tm, d), lambda i, kk: (i, 0)),
                   pl.BlockSpec((tm, d), lambda i, kk: (i, 0))],
        out_shape=[jax.ShapeDtypeStruct((r, d), F32), jax.ShapeDtypeStruct((r, d), BF16)],
        scratch_shapes=[pltpu.VMEM((tm, d), F32)],
        compiler_params=_cparams(("parallel", "arbitrary")),
        name="mm_res_ln",
    )(x, w, res, g.reshape(1, d), b.reshape(1, d))


def _ffn_up_kernel(x_ref, wg_ref, wu_ref, o_ref):
    x = x_ref[...]
    a = _dot(x, wg_ref[...])
    o_ref[...] = (a * jax.nn.sigmoid(a) * _dot(x, wu_ref[...])).astype(o_ref.dtype)


def _ffn_up(x, wg, wu):
    r, d = x.shape
    f = wg.shape[1]
    tm, tn = _row_tile(r, TM_MM), 512
    return pl.pallas_call(
        _ffn_up_kernel,
        grid=(f // tn, r // tm),
        in_specs=[pl.BlockSpec((tm, d), lambda j, i: (i, 0)),
                  pl.BlockSpec((d, tn), lambda j, i: (0, j)),
                  pl.BlockSpec((d, tn), lambda j, i: (0, j))],
        out_specs=pl.BlockSpec((tm, tn), lambda j, i: (i, j)),
        out_shape=jax.ShapeDtypeStruct((r, f), BF16),
        compiler_params=_cparams(("parallel", "parallel")),
        name="ffn_up",
    )(x, wg, wu)


def _rope_table(positions, scale):
    half = QK_ROPE // 2
    inv = jnp.power(ROPE_BASE, -jnp.arange(half, dtype=F32) / half)
    ang = positions.astype(F32)[:, None] * inv[None, :]
    cos, sin = jnp.cos(ang), jnp.sin(ang)
    return jnp.concatenate([cos, cos, -sin, sin], axis=-1) * scale


def _rope_cols(w):
    half = QK_ROPE // 2
    x1, x2 = w[:, :half], w[:, half:]
    return jnp.concatenate([x1, x2, x2, x1], axis=1)


def _split_w_in(w, d):
    qk = QK_NOPE + QK_ROPE
    o_q = HEADS * qk
    o_ckv = o_q + KV_RANK
    o_kr = o_ckv + QK_ROPE
    o_conv = o_kr + 2 * d
    o_u = o_conv + d
    wq = w[:, :o_q].reshape(d, HEADS, qk)
    slots = [jnp.concatenate([wq[:, h, :QK_NOPE], _rope_cols(wq[:, h, QK_NOPE:])], axis=1)
             for h in range(HEADS)]
    w_q = jnp.concatenate(slots, axis=1)
    w_ckv = jnp.concatenate([w[:, o_q:o_ckv], _rope_cols(w[:, o_ckv:o_kr])], axis=1)
    w_conv = w[:, o_kr:o_conv]
    w_u = w[:, o_conv:o_u]
    w_g = w[:, o_u:]
    return tuple(a.astype(BF16) for a in (w_q, w_ckv, w_conv, w_u, w_g))


def _split_w_ukv(w):
    wr = w.reshape(KV_RANK, HEADS, QK_NOPE + V_DIM)
    w_k = wr[:, :, :QK_NOPE].reshape(KV_RANK, HEADS * QK_NOPE)
    w_vt = wr[:, :, QK_NOPE:].reshape(KV_RANK, HEADS * V_DIM).T
    return w_k.astype(BF16), w_vt.astype(BF16)


def kernel(x, meta_tokens, ln0_g, ln0_b, w_in, b_gate, kv_norm_g, w_ukv, w_mla_out, conv_w, conv_b, conv_ln_g, conv_ln_b, w_conv_out, ssm_lam_re, ssm_lam_im, ssm_log_dt, ssm_b_re, ssm_b_im, ssm_c_re, ssm_c_im, ssm_d, w_glu, w_out, ln1_g, ln1_b, w_ffn_gate, w_ffn_up, w_ffn_down, ln2_g, ln2_b):
    bsz, seq, d = x.shape
    depth = w_in.shape[0]
    rows_real = bsz * seq
    assert seq % BQ == 0 and seq % TT_CONV == 0 and META_TILE % TM_LN == 0
    alpha = (2 * depth) ** 0.25

    x_rows = jnp.concatenate([x.reshape(rows_real, d), meta_tokens.astype(x.dtype),
                              jnp.zeros((META_TILE - N_META, d), x.dtype)], axis=0)
    pos = jnp.concatenate([jnp.tile(N_META + jnp.arange(seq), bsz), jnp.arange(META_TILE)])
    q_tbl = _rope_table(pos, Q_SCALE)
    k_tbl = _rope_table(pos, 1.0)
    n_scan = (seq // SSM_CHUNK).bit_length() - 1
    prep = _ssm_prep(ssm_lam_re, ssm_lam_im, ssm_log_dt, ssm_b_re, ssm_b_im, ssm_c_re, ssm_c_im,
                     n_scan)

    h, hb = _ln0(x_rows, ln0_g, ln0_b)
    for l in range(depth):
        w_q, w_ckv, w_conv, w_u, w_g = _split_w_in(w_in[l], d)
        q = _matmul(hb, w_q, tn=4 * HEAD_SLOT, rope_tbl=q_tbl, name="in_proj_q")
        ckv = _matmul(hb, w_ckv, tn=CKV_COLS, name="in_proj_ckv")
        pc = _matmul(hb, w_conv, tn=1024, name="in_proj_conv")
        u_nat = _matmul(hb, w_u, tn=1024, out_dtype=F32, name="in_proj_u")
        gates = _matmul(hb, w_g, tn=1024, bias=b_gate[l], name="in_proj_gates")

        w_k, w_vt = _split_w_ukv(w_ukv[l])
        k, vt = _kv_up(ckv, kv_norm_g[l], w_k, w_vt, k_tbl)
        o = _attention(q, k, vt, bsz, seq)
        o = _attention_meta(q, k, vt, o, rows_real)
        zc = _conv_branch(pc, conv_w[l], conv_b[l], conv_ln_g[l], conv_ln_b[l], rows_real, seq)
        gy = _ssm_branch(u_nat, prep, ssm_d[l:l + 1], l, bsz, seq)

        mix = _mix(o, zc, gy, gates, w_mla_out[l].astype(BF16), w_conv_out[l].astype(BF16),
                   w_glu[l].astype(BF16))
        h, hb = _mm_res_ln(mix, w_out[l].astype(BF16), h, ln1_g[l], ln1_b[l], alpha=alpha, tk=d)
        ff = _ffn_up(hb, w_ffn_gate[l].astype(BF16), w_ffn_up[l].astype(BF16))
        h, hb = _mm_res_ln(ff, w_ffn_down[l].astype(BF16), h, ln2_g[l], ln2_b[l], alpha=alpha,
                           tk=w_ffn_down.shape[1] // 4)
    return h[:rows_real].reshape(bsz, seq, d)
```

```python
import functools
import math

import jax
import jax.numpy as jnp
from jax import lax
from jax.experimental import pallas as pl
from jax.experimental.pallas import tpu as pltpu

F32 = jnp.float32
BF16 = jnp.bfloat16

N_META = 16
CHUNK = 64
HEADS = 16
QK_NOPE = 128
QK_ROPE = 64
V_DIM = 128
KV_RANK = 512
ROPE_BASE = 10000.0
MLA_SCALE = (QK_NOPE + QK_ROPE) ** -0.5
LOG2E = math.log2(math.e)
Q_SCALE = MLA_SCALE * LOG2E
CONV_K = 31
SSM_GROUP = 16
SSM_STATE = 64
LN_EPS = 1e-5
RMS_EPS = 1e-6

LANES = 128
HEAD_SLOT = 2 * LANES
CKV_COLS = KV_RANK + LANES
META_TILE = 512
CONV_HALO = 32
VMEM_LIMIT = 56 * 1024 * 1024

TM_MM = (1280, 1024, 768, 512)
TM_LN = 512
TM_KV = 512
TT_CONV = 256
BQ = 1024
ATT_HEADS = 4
ATT_SUB = 256
SSM_CHUNK = 32
SSM_GB = LANES // SSM_GROUP


def _row_tile(rows, candidates):
    return next(t for t in candidates if rows % t == 0)


def _cparams(sem):
    return pltpu.CompilerParams(dimension_semantics=sem, vmem_limit_bytes=VMEM_LIMIT)


def _dot(a, b):
    return jnp.dot(a, b, preferred_element_type=F32)


def _dot_nt(a, b):
    return lax.dot_general(a, b, (((1,), (1,)), ((), ())), preferred_element_type=F32)


def _dot_exact(a, b):
    return jnp.dot(a, b, preferred_element_type=F32, precision=lax.Precision.HIGHEST)


def _layer_norm(x, g, b):
    mu = jnp.mean(x, axis=-1, keepdims=True)
    xc = x - mu
    var = jnp.mean(xc * xc, axis=-1, keepdims=True)
    return xc * lax.rsqrt(var + LN_EPS) * g + b


def _rope_slot(a, tbl):
    p = a * tbl
    return p + pltpu.roll(p, 64, 1)


def _ln0_kernel(x_ref, g_ref, b_ref, h_ref, hb_ref):
    y = _layer_norm(x_ref[...], g_ref[...], b_ref[...])
    h_ref[...] = y
    hb_ref[...] = y.astype(BF16)


def _ln0(x_rows, g, b):
    r, d = x_rows.shape
    tm = TM_LN
    return pl.pallas_call(
        _ln0_kernel,
        grid=(r // tm,),
        in_specs=[pl.BlockSpec((tm, d), lambda i: (i, 0)),
                  pl.BlockSpec((1, d), lambda i: (0, 0)),
                  pl.BlockSpec((1, d), lambda i: (0, 0))],
        out_specs=[pl.BlockSpec((tm, d), lambda i: (i, 0)),
                   pl.BlockSpec((tm, d), lambda i: (i, 0))],
        out_shape=[jax.ShapeDtypeStruct((r, d), F32), jax.ShapeDtypeStruct((r, d), BF16)],
        compiler_params=_cparams(("parallel",)),
        name="ln0",
    )(x_rows, g.reshape(1, d), b.reshape(1, d))


def _mm_kernel(x_ref, w_ref, o_ref):
    o_ref[...] = _dot(x_ref[...], w_ref[...]).astype(o_ref.dtype)


def _mm_sigmoid_kernel(x_ref, w_ref, b_ref, o_ref):
    o_ref[...] = jax.nn.sigmoid(_dot(x_ref[...], w_ref[...]) + b_ref[...]).astype(o_ref.dtype)


def _mm_qrope_kernel(x_ref, w_ref, t_ref, o_ref, *, heads_per_tile):
    acc = _dot(x_ref[...], w_ref[...])
    tbl = t_ref[...]
    for hh in range(heads_per_tile):
        c0 = hh * HEAD_SLOT
        o_ref[:, c0:c0 + LANES] = (acc[:, c0:c0 + LANES] * Q_SCALE).astype(o_ref.dtype)
        o_ref[:, c0 + LANES:c0 + HEAD_SLOT] = _rope_slot(
            acc[:, c0 + LANES:c0 + HEAD_SLOT], tbl).astype(o_ref.dtype)


def _matmul(x, w, *, tn, bias=None, rope_tbl=None, out_dtype=BF16, name):
    r, k = x.shape
    n = w.shape[1]
    tm = _row_tile(r, TM_MM)
    in_specs = [pl.BlockSpec((tm, k), lambda j, i: (i, 0)),
                pl.BlockSpec((k, tn), lambda j, i: (0, j))]
    args = [x, w]
    if bias is not None:
        kern = _mm_sigmoid_kernel
        in_specs.append(pl.BlockSpec((1, tn), lambda j, i: (0, j)))
        args.append(bias.reshape(1, n))
    elif rope_tbl is not None:
        kern = functools.partial(_mm_qrope_kernel, heads_per_tile=tn // HEAD_SLOT)
        in_specs.append(pl.BlockSpec((tm, LANES), lambda j, i: (i, 0)))
        args.append(rope_tbl)
    else:
        kern = _mm_kernel
    return pl.pallas_call(
        kern,
        grid=(n // tn, r // tm),
        in_specs=in_specs,
        out_specs=pl.BlockSpec((tm, tn), lambda j, i: (i, j)),
        out_shape=jax.ShapeDtypeStruct((r, n), out_dtype),
        compiler_params=_cparams(("parallel", "parallel")),
        name=name,
    )(*args)


def _kv_up_kernel(c_ref, g_ref, wk_ref, wvt_ref, t_ref, k_ref, vt_ref):
    c = c_ref[:, :KV_RANK].astype(F32)
    ms = jnp.mean(c * c, axis=-1, keepdims=True)
    cn = (c * lax.rsqrt(ms + RMS_EPS) * g_ref[...]).astype(BF16)
    kn = _dot(cn, wk_ref[...])
    vt_ref[...] = _dot_nt(wvt_ref[...], cn).astype(BF16)
    kr = _rope_slot(c_ref[:, KV_RANK:].astype(F32), t_ref[...])
    lane = lax.broadcasted_iota(jnp.int32, kr.shape, 1)
    kr = jnp.where(lane < QK_ROPE, kr, 0.0).astype(BF16)
    for h in range(HEADS):
        k_ref[:, h * HEAD_SLOT:h * HEAD_SLOT + LANES] = kn[:, h * QK_NOPE:(h + 1) * QK_NOPE].astype(BF16)
        k_ref[:, h * HEAD_SLOT + LANES:(h + 1) * HEAD_SLOT] = kr


def _kv_up(ckv, g, w_k, w_vt, k_tbl):
    r = ckv.shape[0]
    tm = TM_KV
    nk, nv = HEADS * HEAD_SLOT, HEADS * V_DIM
    return pl.pallas_call(
        _kv_up_kernel,
        grid=(r // tm,),
        in_specs=[pl.BlockSpec((tm, CKV_COLS), lambda i: (i, 0)),
                  pl.BlockSpec((1, KV_RANK), lambda i: (0, 0)),
                  pl.BlockSpec(w_k.shape, lambda i: (0, 0)),
                  pl.BlockSpec(w_vt.shape, lambda i: (0, 0)),
                  pl.BlockSpec((tm, LANES), lambda i: (i, 0))],
        out_specs=[pl.BlockSpec((tm, nk), lambda i: (i, 0)),
                   pl.BlockSpec((nv, tm), lambda i: (0, i))],
        out_shape=[jax.ShapeDtypeStruct((r, nk), BF16), jax.ShapeDtypeStruct((nv, r), BF16)],
        compiler_params=_cparams(("parallel",)),
        name="kv_up",
    )(ckv, g.reshape(1, KV_RANK), w_k, w_vt, k_tbl)


def _attn_kernel(qi_ref, kj_ref, q_ref, k_ref, vt_ref, km_ref, vmt_ref, o_ref, m_ref, l_ref, acc_ref):
    step = pl.program_id(2)
    qi = qi_ref[step]
    kj = kj_ref[step]
    sub = ATT_SUB
    nsub = q_ref.shape[0] // sub

    def cols(j):
        return slice(j * sub, (j + 1) * sub)

    def head_cols(hh, width):
        return slice(hh * width, (hh + 1) * width)

    def load_state(hh):
        return ([m_ref[hh, :, cols(j)] for j in range(nsub)], [l_ref[hh, :, cols(j)] for j in range(nsub)],
                [acc_ref[hh, :, cols(j)] for j in range(nsub)])

    def store_state(hh, m, l, acc):
        for j in range(nsub):
            m_ref[hh, :, cols(j)] = m[j]
            l_ref[hh, :, cols(j)] = l[j]
            acc_ref[hh, :, cols(j)] = acc[j]

    def attend(key_blocks, diagonal):
        scores, maxes = [], []
        for hh in range(ATT_HEADS):
            m, _, _ = load_state(hh)
            st_h = []
            for c, (keys, _, mask) in enumerate(key_blocks):
                j0 = c if diagonal else 0
                st = _dot_nt(keys(hh), q_ref[j0 * sub:, head_cols(hh, HEAD_SLOT)])
                pieces = []
                for j in range(j0, nsub):
                    sj = st[:, (j - j0) * sub:(j - j0 + 1) * sub]
                    if mask is not None:
                        sj = mask(sj)
                    elif diagonal and j == c:
                        kc = lax.broadcasted_iota(jnp.int32, sj.shape, 0) // CHUNK
                        qc = lax.broadcasted_iota(jnp.int32, sj.shape, 1) // CHUNK
                        sj = jnp.where(kc <= qc, sj, -jnp.inf)
                    m[j] = jnp.maximum(m[j], jnp.max(sj, axis=0, keepdims=True))
                    pieces.append(sj)
                st_h.append(pieces)
            scores.append(st_h)
            maxes.append(m)
        for hh in range(ATT_HEADS):
            m_old, l, acc = load_state(hh)
            m = maxes[hh]
            alpha = [jnp.exp2(m_old[j] - m[j]) for j in range(nsub)]
            l = [alpha[j] * l[j] for j in range(nsub)]
            acc = [alpha[j] * acc[j] for j in range(nsub)]
            for c, (_, values, _) in enumerate(key_blocks):
                j0 = c if diagonal else 0
                vt = values(hh)
                for j in range(j0, nsub):
                    p = jnp.exp2(scores[hh][c][j - j0] - m[j])
                    l[j] = l[j] + jnp.sum(p, axis=0, keepdims=True)
                    acc[j] = acc[j] + _dot(vt, p.astype(BF16))
            store_state(hh, m, l, acc)

    @pl.when(kj == 0)
    def _():
        for hh in range(ATT_HEADS):
            m_ref[hh] = jnp.full(m_ref.shape[1:], -jnp.inf, F32)
            l_ref[hh] = jnp.zeros(l_ref.shape[1:], F32)
            acc_ref[hh] = jnp.zeros(acc_ref.shape[1:], F32)

        def meta_mask(st):
            key = lax.broadcasted_iota(jnp.int32, st.shape, 0)
            return jnp.where(key < N_META, st, -jnp.inf)

        attend([(lambda hh: km_ref[:, head_cols(hh, HEAD_SLOT)],
                 lambda hh: vmt_ref[head_cols(hh, V_DIM), :], meta_mask)], False)

    def key_blocks():
        return [(lambda hh, c=c: k_ref[c * sub:(c + 1) * sub, head_cols(hh, HEAD_SLOT)],
                 lambda hh, c=c: vt_ref[head_cols(hh, V_DIM), c * sub:(c + 1) * sub], None)
                for c in range(nsub)]

    @pl.when(kj < qi)
    def _():
        attend(key_blocks(), False)

    @pl.when(kj == qi)
    def _():
        attend(key_blocks(), True)
        for hh in range(ATT_HEADS):
            for j in range(nsub):
                o = acc_ref[hh, :, cols(j)] / l_ref[hh, :, cols(j)]
                o_ref[cols(j), head_cols(hh, V_DIM)] = o.T.astype(o_ref.dtype)


def _attention(q, k, vt, bsz, seq):
    r = bsz * seq
    bq = BQ
    nq = seq // bq
    nh = ATT_HEADS
    pairs = [(i, j) for i in range(nq) for j in range(i + 1)]
    qi = jnp.asarray([p[0] for p in pairs], jnp.int32)
    kj = jnp.asarray([p[1] for p in pairs], jnp.int32)
    meta_blk = (bsz * seq) // LANES
    grid_spec = pltpu.PrefetchScalarGridSpec(
        num_scalar_prefetch=2,
        grid=(bsz, HEADS // nh, len(pairs)),
        in_specs=[
            pl.BlockSpec((bq, nh * HEAD_SLOT), lambda b, h, s, qi, kj: (b * nq + qi[s], h)),
            pl.BlockSpec((bq, nh * HEAD_SLOT), lambda b, h, s, qi, kj: (b * nq + kj[s], h)),
            pl.BlockSpec((nh * V_DIM, bq), lambda b, h, s, qi, kj: (h, b * nq + kj[s])),
            pl.BlockSpec((LANES, nh * HEAD_SLOT), lambda b, h, s, qi, kj: (meta_blk, h)),
            pl.BlockSpec((nh * V_DIM, LANES), lambda b, h, s, qi, kj: (h, meta_blk)),
        ],
        out_specs=pl.BlockSpec((bq, nh * V_DIM), lambda b, h, s, qi, kj: (b * nq + qi[s], h)),
        scratch_shapes=[pltpu.VMEM((nh, 1, bq), F32), pltpu.VMEM((nh, 1, bq), F32),
                        pltpu.VMEM((nh, V_DIM, bq), F32)],
    )
    return pl.pallas_call(
        _attn_kernel,
        grid_spec=grid_spec,
        out_shape=jax.ShapeDtypeStruct((r, HEADS * V_DIM), BF16),
        compiler_params=_cparams(("parallel", "parallel", "arbitrary")),
        name="mla_attn",
    )(qi, kj, q, k, vt, k, vt)


def _attn_meta_kernel(q_ref, km_ref, vmt_ref, o_ref):
    s = _dot_nt(q_ref[...], km_ref[...])
    lane = lax.broadcasted_iota(jnp.int32, s.shape, 1)
    s = jnp.where(lane < N_META, s, -jnp.inf)
    m = jnp.max(s, axis=-1, keepdims=True)
    p = jnp.exp2(s - m)
    l = jnp.sum(p, axis=-1, keepdims=True)
    o_ref[...] = (_dot_nt(p.astype(BF16), vmt_ref[...]) / l).astype(o_ref.dtype)


def _attention_meta(q, k, vt, rows_real):
    meta_blk = rows_real // LANES
    tile_blk = rows_real // META_TILE
    return pl.pallas_call(
        _attn_meta_kernel,
        grid=(HEADS,),
        in_specs=[pl.BlockSpec((META_TILE, HEAD_SLOT), lambda h: (tile_blk, h)),
                  pl.BlockSpec((LANES, HEAD_SLOT), lambda h: (meta_blk, h)),
                  pl.BlockSpec((V_DIM, LANES), lambda h: (h, meta_blk))],
        out_specs=pl.BlockSpec((META_TILE, V_DIM), lambda h: (0, h)),
        out_shape=jax.ShapeDtypeStruct((META_TILE, HEADS * V_DIM), BF16),
        compiler_params=_cparams(("parallel",)),
        name="mla_attn_meta",
    )(q, k, vt)


CONV_RB = 64
CONV_LB = 256


def _conv_kernel(val_ref, gate_ref, hval_ref, hgate_ref, mval_ref, mgate_ref, w_ref, cb_ref,
                 g_ref, b_ref, o_ref, z_ref, c_ref, *, tt, tiles_per_seq, n_real_tiles):
    i = pl.program_id(0)
    chans = o_ref.shape[1]

    def glu(v, g):
        return v.astype(F32) * jax.nn.sigmoid(g.astype(F32))

    z_ref[CONV_HALO:CONV_HALO + tt, :] = glu(val_ref[...], gate_ref[...])
    z_ref[CONV_HALO + tt:, :] = jnp.zeros((8, chans), F32)
    is_meta = i >= n_real_tiles
    first = jnp.logical_and(i % tiles_per_seq == 0, jnp.logical_not(is_meta))
    z_prev = glu(hval_ref[...], hgate_ref[...])
    z_meta = glu(mval_ref[...], mgate_ref[...])
    pad = CONV_HALO - N_META
    z_start = jnp.concatenate([jnp.zeros((pad, chans), F32), z_meta[:N_META]], axis=0)
    z_ref[:CONV_HALO, :] = jnp.where(is_meta, 0.0, jnp.where(first, z_start, z_prev))

    off = CONV_HALO - (CONV_K - 1)
    for rb in range(tt // CONV_RB):
        for lb in range(chans // CONV_LB):
            cs = slice(lb * CONV_LB, (lb + 1) * CONV_LB)
            acc = jnp.zeros((CONV_RB, CONV_LB), F32)
            for res in range(8):
                taps = [k for k in range(CONV_K) if (off + k) % 8 == res]
                base = rb * CONV_RB + (off + taps[0]) - res
                part = jnp.zeros((CONV_RB + 8, CONV_LB), F32)
                for k in taps:
                    r0 = base + (k - taps[0])
                    part = part + z_ref[r0:r0 + CONV_RB + 8, cs] * w_ref[k:k + 1, cs]
                acc = acc + part[res:res + CONV_RB]
            c_ref[rb * CONV_RB:(rb + 1) * CONV_RB, cs] = acc + cb_ref[:, cs]
    y = _layer_norm(c_ref[...], g_ref[...], b_ref[...])
    o_ref[...] = (y * jax.nn.sigmoid(y)).astype(o_ref.dtype)


def _conv_branch(pc, conv_w, conv_b, ln_g, ln_b, rows_real, seq):
    r = pc.shape[0]
    chans = conv_w.shape[1]
    tt = TT_CONV
    hb = tt // CONV_HALO
    kern = functools.partial(_conv_kernel, tt=tt, tiles_per_seq=seq // tt,
                             n_real_tiles=rows_real // tt)
    meta_blk = rows_real // CONV_HALO

    def halo(i):
        return jnp.maximum(i * hb - 1, 0)

    row = lambda a: a.reshape(1, chans)
    return pl.pallas_call(
        kern,
        grid=(r // tt,),
        in_specs=[pl.BlockSpec((tt, chans), lambda i: (i, 0)),
                  pl.BlockSpec((tt, chans), lambda i: (i, 1)),
                  pl.BlockSpec((CONV_HALO, chans), lambda i: (halo(i), 0)),
                  pl.BlockSpec((CONV_HALO, chans), lambda i: (halo(i), 1)),
                  pl.BlockSpec((CONV_HALO, chans), lambda i: (meta_blk, 0)),
                  pl.BlockSpec((CONV_HALO, chans), lambda i: (meta_blk, 1)),
                  pl.BlockSpec((CONV_K, chans), lambda i: (0, 0)),
                  pl.BlockSpec((1, chans), lambda i: (0, 0)),
                  pl.BlockSpec((1, chans), lambda i: (0, 0)),
                  pl.BlockSpec((1, chans), lambda i: (0, 0))],
        out_specs=pl.BlockSpec((tt, chans), lambda i: (i, 0)),
        out_shape=jax.ShapeDtypeStruct((r, chans), BF16),
        scratch_shapes=[pltpu.VMEM((CONV_HALO + tt + 8, chans), F32), pltpu.VMEM((tt, chans), F32)],
        compiler_params=_cparams(("parallel",)),
        name="conv_branch",
    )(pc, pc, pc, pc, pc, pc, conv_w, row(conv_b), row(ln_g), row(ln_b))


SSM_SL = SSM_CHUNK * SSM_GROUP


def _ssm_prep_kernel(lc_ref, lr_ref, b_ref, c_ref, cc_ref, e_ref, et_ref, tt_ref, pt_ref, qst_ref,
                     sc_ref, *, n_steps):
    ns, ch = SSM_STATE, SSM_CHUNK

    def power(lre_dt, ang, e):
        mag = jnp.exp(lre_dt * e)
        return mag * jnp.cos(ang * e), mag * jnp.sin(ang * e)

    lc = lc_ref[0]
    lam_re, lam_im = lc[:, 0:1], lc[:, 1:2]
    dt = jnp.exp(lc[:, 2:3])
    lre_dt, ang = lam_re * dt, lam_im * dt
    a_re, a_im = power(lre_dt, ang, 1.0)
    den = lam_re * lam_re + lam_im * lam_im
    f_re = (((a_re - 1.0) * lam_re + a_im * lam_im) / den)[:ns]
    f_im = ((a_im * lam_re - (a_re - 1.0) * lam_im) / den)[:ns]
    step = (lax.broadcasted_iota(jnp.int32, (1, SSM_SL), 1) // SSM_GROUP).astype(F32)
    ae_re, ae_im = power(lre_dt[:ns], ang[:ns], (ch - 1.0) - step)
    expand = e_ref[...]
    b_re = _dot_exact(b_ref[0, 0], expand)
    b_im = _dot_exact(b_ref[0, 1], expand)
    bb_re = f_re * b_re - f_im * b_im
    bb_im = f_re * b_im + f_im * b_re
    p_re = ae_re * bb_re - ae_im * bb_im
    p_im = ae_re * bb_im + ae_im * bb_re
    pt_ref[0, :ns, :] = p_re.astype(pt_ref.dtype)
    pt_ref[0, ns:, :] = p_im.astype(pt_ref.dtype)
    rev = _dot_exact(c_ref[0, 0], p_re) - _dot_exact(c_ref[0, 1], p_im)
    lane_t = lax.broadcasted_iota(jnp.int32, rev.shape, 1)
    for t in range(ch):
        w = (t + 1) * SSM_GROUP
        rolled = rev if w == SSM_SL else pltpu.roll(rev, w, 1)
        tt_ref[0, t * SSM_GROUP:(t + 1) * SSM_GROUP, :] = jnp.where(
            lane_t < w, rolled, 0.0).astype(tt_ref.dtype)
    lane2 = lax.broadcasted_iota(jnp.int32, (LANES, LANES), 1)
    row2 = lax.broadcasted_iota(jnp.int32, (LANES, LANES), 0)
    expo = (ch * jnp.left_shift(1, jnp.minimum(lane2[0:1] // 2, n_steps))).astype(F32)
    s_re, s_im = power(lre_dt, ang, expo)
    sc_ref[0] = jnp.where(lane2 % 2 == 0, s_re, jnp.where(row2 < ns, -s_im, s_im))

    lr = lr_ref[0]
    dt_r = jnp.exp(lr[2:3])
    tp1 = (lax.broadcasted_iota(jnp.int32, (SSM_SL, 1), 0) // SSM_GROUP + 1).astype(F32)
    ap_re, ap_im = power(lr[0:1] * dt_r, lr[1:2] * dt_r, tp1)
    ctile = _dot_exact(et_ref[...], cc_ref[0])
    sign = jnp.where(lane2[0:1] < ns, 1.0, -1.0)
    qst_ref[0] = (sign * (ap_re * ctile) - pltpu.roll(ap_im * ctile, ns, 1)).astype(qst_ref.dtype)


def _ssm_prep(lam_re, lam_im, log_dt, b_re, b_im, c_re, c_im, n_steps):
    nl, ng, ns = lam_re.shape
    lg = nl * ng
    dtb = jnp.broadcast_to(log_dt[..., None], lam_re.shape)
    lrow = jnp.stack([lam_re, lam_im, dtb], axis=-2)
    lrow = jnp.concatenate([lrow, lrow], axis=-1).reshape(lg, 3, 2 * ns)
    lcol = jnp.swapaxes(lrow, -1, -2)
    b = jnp.stack([b_re, b_im], axis=2).reshape(lg, 2, ns, SSM_GROUP)
    c = jnp.stack([c_re, c_im], axis=2).reshape(lg, 2, SSM_GROUP, ns)
    ccat = jnp.concatenate([c_re, c_im], axis=-1).reshape(lg, SSM_GROUP, 2 * ns)
    lane_c = jnp.arange(SSM_SL) % SSM_GROUP
    expand = (lane_c[None, :] == jnp.arange(SSM_GROUP)[:, None]).astype(F32)
    blk = lambda *shape: pl.BlockSpec((1,) + shape, lambda i: (i,) + (0,) * len(shape))
    return pl.pallas_call(
        functools.partial(_ssm_prep_kernel, n_steps=n_steps),
        grid=(lg,),
        in_specs=[blk(2 * ns, 3), blk(3, 2 * ns), blk(2, ns, SSM_GROUP), blk(2, SSM_GROUP, ns),
                  blk(SSM_GROUP, 2 * ns),
                  pl.BlockSpec((SSM_GROUP, SSM_SL), lambda i: (0, 0)),
                  pl.BlockSpec((SSM_SL, SSM_GROUP), lambda i: (0, 0))],
        out_specs=[blk(SSM_SL, SSM_SL), blk(2 * ns, SSM_SL), blk(SSM_SL, 2 * ns), blk(2 * ns, 2 * ns)],
        out_shape=[jax.ShapeDtypeStruct((lg, SSM_SL, SSM_SL), BF16),
                   jax.ShapeDtypeStruct((lg, 2 * ns, SSM_SL), BF16),
                   jax.ShapeDtypeStruct((lg, SSM_SL, 2 * ns), BF16),
                   jax.ShapeDtypeStruct((lg, 2 * ns, 2 * ns), F32)],
        compiler_params=_cparams(("parallel",)),
        name="ssm_prep",
    )(lcol, lrow, b, c, ccat, expand, expand.T)


def _swap_halves(x):
    return pltpu.roll(x, SSM_STATE, 0)


def _ssm_kernel(u_ref, tt_ref, pt_ref, qst_ref, sc_ref, d_ref, um_ref, o_ref, ym_ref, ut_ref, yt_ref,
                *, nk, n_steps):
    ch = SSM_CHUNK
    for s in range(ch):
        ust = u_ref[pl.ds(s, nk, stride=ch), :].T
        for g in range(SSM_GB):
            ut_ref[g, s * SSM_GROUP:(s + 1) * SSM_GROUP, :] = (
                ust[g * SSM_GROUP:(g + 1) * SSM_GROUP, :].astype(BF16))
    lane = lax.broadcasted_iota(jnp.int32, (LANES, nk), 1)
    for g in range(SSM_GB):
        ut = ut_ref[g]
        um = um_ref[g]
        tt, pt, sc = tt_ref[g], pt_ref[g], sc_ref[g]
        ym_ref[g] = _dot_nt(um, tt)[:8]
        g0 = _dot_nt(pt, um)[:, 0:1]
        seed = sc[:, 0:1] * g0 + sc[:, 1:2] * _swap_halves(g0)
        e = _dot(pt, ut) + jnp.where(lane == 0, seed, 0.0)
        for i in range(n_steps):
            sh = jnp.where(lane >= 2 ** i, pltpu.roll(e, 2 ** i, 1), 0.0)
            e = e + sc[:, 2 * i:2 * i + 1] * sh + sc[:, 2 * i + 1:2 * i + 2] * _swap_halves(sh)
        state_in = jnp.where(lane == 0, g0, pltpu.roll(e, 1, 1))
        yt_ref[g] = _dot(tt, ut) + _dot(qst_ref[g], state_in.astype(BF16))
    for t in range(ch):
        yt = jnp.concatenate([yt_ref[g, t * SSM_GROUP:(t + 1) * SSM_GROUP, :]
                              for g in range(SSM_GB)], axis=0)
        y = yt.T + d_ref[...] * u_ref[pl.ds(t, nk, stride=ch), :]
        o_ref[pl.ds(t, nk, stride=ch), :] = jax.nn.gelu(y, approximate=True)


def _ssm_meta_kernel(ym_ref, um_ref, d_ref, o_ref):
    o_ref[N_META:, :] = jnp.zeros((o_ref.shape[0] - N_META, o_ref.shape[1]), o_ref.dtype)
    y = ym_ref[...] + d_ref[...] * um_ref[...]
    o_ref[:N_META, :] = jax.nn.gelu(y, approximate=True)


def _ssm_branch(u_nat, prep, d_row, layer, bsz, seq):
    tt_all, pt_all, qst_all, sc_all = prep
    chans = u_nat.shape[1]
    ng = chans // SSM_GROUP
    nj = ng // SSM_GB
    rows_real = bsz * seq
    ch = SSM_CHUNK
    nk = seq // ch
    n_steps = nk.bit_length() - 1
    assert nk == 2 ** n_steps and nk % LANES == 0 and ch >= N_META
    u_meta = u_nat[rows_real:rows_real + N_META]
    um = u_meta.reshape(N_META, ng, SSM_GROUP).transpose(1, 0, 2).reshape(ng, N_META * SSM_GROUP)
    um = jnp.pad(um, ((0, 0), ((ch - N_META) * SSM_GROUP, 0)))
    um = jnp.pad(um[:, None, :], ((0, 0), (0, LANES - 1), (0, 0))).astype(BF16)
    j0 = layer * nj
    blk = lambda *shape: pl.BlockSpec((SSM_GB,) + shape, lambda j, b: (j0 + j,) + (0,) * len(shape))
    kern = functools.partial(_ssm_kernel, nk=nk, n_steps=n_steps)
    gy, ym = pl.pallas_call(
        kern,
        grid=(nj, bsz),
        in_specs=[pl.BlockSpec((seq, LANES), lambda j, b: (b, j)),
                  blk(SSM_SL, SSM_SL), blk(LANES, SSM_SL), blk(SSM_SL, LANES), blk(LANES, LANES),
                  pl.BlockSpec((1, LANES), lambda j, b: (0, j)),
                  pl.BlockSpec((SSM_GB, LANES, SSM_SL), lambda j, b: (j, 0, 0))],
        out_specs=[pl.BlockSpec((seq, LANES), lambda j, b: (b, j)),
                   pl.BlockSpec((SSM_GB, 8, SSM_SL), lambda j, b: (j, 0, 0))],
        out_shape=[jax.ShapeDtypeStruct((rows_real, chans), F32),
                   jax.ShapeDtypeStruct((ng, 8, SSM_SL), F32)],
        scratch_shapes=[pltpu.VMEM((SSM_GB, SSM_SL, nk), BF16), pltpu.VMEM((SSM_GB, SSM_SL, nk), F32)],
        compiler_params=_cparams(("parallel", "arbitrary")),
        name="ssm_scan",
    )(u_nat, tt_all, pt_all, qst_all, sc_all, d_row, um)
    y_meta = ym[:, 0, :].reshape(ng, ch, SSM_GROUP)[:, ch - N_META:, :]
    y_meta = y_meta.transpose(1, 0, 2).reshape(N_META, chans)
    gy_meta = pl.pallas_call(
        _ssm_meta_kernel,
        grid=(1,),
        in_specs=[pl.BlockSpec((N_META, chans), lambda i: (0, 0)),
                  pl.BlockSpec((N_META, chans), lambda i: (0, 0)),
                  pl.BlockSpec((1, chans), lambda i: (0, 0))],
        out_specs=pl.BlockSpec((META_TILE, chans), lambda i: (0, 0)),
        out_shape=jax.ShapeDtypeStruct((META_TILE, chans), F32),
        compiler_params=_cparams(("arbitrary",)),
        name="ssm_meta",
    )(y_meta, u_meta, d_row)
    return gy, gy_meta


def _mix_kernel(o_ref, om_ref, zc_ref, gy_ref, gym_ref, ga_ref, gb_ref, gc_ref, wa_ref, wb_ref,
                wza_ref, wzb_ref, out_ref, *, n_real):
    is_meta = pl.program_id(1) == n_real
    ya = _dot(jnp.where(is_meta, om_ref[...], o_ref[...]), wa_ref[...])
    yb = _dot(zc_ref[...], wb_ref[...])
    gy = jnp.where(is_meta, gym_ref[...], gy_ref[...]).astype(BF16)
    yc = _dot(gy, wza_ref[...]) * jax.nn.sigmoid(_dot(gy, wzb_ref[...]))
    mix = (ga_ref[...].astype(F32) * ya + gb_ref[...].astype(F32) * yb
           + gc_ref[...].astype(F32) * yc)
    out_ref[...] = mix.astype(out_ref.dtype)


def _mix(o, o_meta, zc, gy, gy_meta, gates, w_mla_out, w_conv_out, w_glu):
    r, d = zc.shape
    tm, tn = META_TILE, 512
    n_real = o.shape[0] // tm
    assert o.shape[0] % tm == 0 and r == o.shape[0] + META_TILE
    nb = d // tn
    act = lambda: pl.BlockSpec((tm, d), lambda j, i: (i, 0))
    real = lambda: pl.BlockSpec((tm, d), lambda j, i: (jnp.minimum(i, n_real - 1), 0))
    meta = lambda: pl.BlockSpec((tm, d), lambda j, i: (0, 0))
    gate = lambda k: pl.BlockSpec((tm, tn), lambda j, i: (i, k * nb + j))
    wgt = lambda k: pl.BlockSpec((d, tn), lambda j, i: (0, k * nb + j))
    return pl.pallas_call(
        functools.partial(_mix_kernel, n_real=n_real),
        grid=(nb, r // tm),
        in_specs=[real(), meta(), act(), real(), meta(), gate(0), gate(1), gate(2),
                  wgt(0), wgt(0), wgt(0), wgt(1)],
        out_specs=pl.BlockSpec((tm, tn), lambda j, i: (i, j)),
        out_shape=jax.ShapeDtypeStruct((r, d), BF16),
        compiler_params=_cparams(("parallel", "parallel")),
        name="gated_mix",
    )(o, o_meta, zc, gy, gy_meta, gates, gates, gates, w_mla_out, w_conv_out, w_glu, w_glu)


def _mm_res_ln_kernel(x_ref, w_ref, r_ref, g_ref, b_ref, h_ref, hb_ref, acc_ref, *, alpha, nk):
    kk = pl.program_id(1)

    @pl.when(kk == 0)
    def _():
        acc_ref[...] = jnp.zeros_like(acc_ref)

    acc_ref[...] += _dot(x_ref[...], w_ref[...])

    @pl.when(kk == nk - 1)
    def _():
        y = _layer_norm(alpha * r_ref[...] + acc_ref[...], g_ref[...], b_ref[...])
        h_ref[...] = y
        hb_ref[...] = y.astype(BF16)


def _mm_res_ln(x, w, res, g, b, *, alpha, tk):
    r, k = x.shape
    d = w.shape[1]
    tm = TM_LN
    nk = k // tk
    kern = functools.partial(_mm_res_ln_kernel, alpha=alpha, nk=nk)
    return pl.pallas_call(
        kern,
        grid=(r // tm, nk),
        in_specs=[pl.BlockSpec((tm, tk), lambda i, kk: (i, kk)),
                  pl.BlockSpec((tk, d), lambda i, kk: (kk, 0)),
                  pl.BlockSpec((tm, d), lambda i, kk: (i, 0)),
                  pl.BlockSpec((1, d), lambda i, kk: (0, 0)),
                  pl.BlockSpec((1, d), lambda i, kk: (0, 0))],
        out_specs=[pl.BlockSpec((tm, d), lambda i, kk: (i, 0)),
                   pl.BlockSpec((tm, d), lambda i, kk: (i, 0))],
        out_shape=[jax.ShapeDtypeStruct((r, d), F32), jax.ShapeDtypeStruct((r, d), BF16)],
        scratch_shapes=[pltpu.VMEM((tm, d), F32)],
        compiler_params=_cparams(("parallel", "arbitrary")),
        name="mm_res_ln",
    )(x, w, res, g.reshape(1, d), b.reshape(1, d))


def _ffn_up_kernel(x_ref, wg_ref, wu_ref, o_ref):
    x = x_ref[...]
    a = _dot(x, wg_ref[...])
    o_ref[...] = (a * jax.nn.sigmoid(a) * _dot(x, wu_ref[...])).astype(o_ref.dtype)


def _ffn_up(x, wg, wu):
    r, d = x.shape
    f = wg.shape[1]
    tm, tn = _row_tile(r, TM_MM), 512
    return pl.pallas_call(
        _ffn_up_kernel,
        grid=(f // tn, r // tm),
        in_specs=[pl.BlockSpec((tm, d), lambda j, i: (i, 0)),
                  pl.BlockSpec((d, tn), lambda j, i: (0, j)),
                  pl.BlockSpec((d, tn), lambda j, i: (0, j))],
        out_specs=pl.BlockSpec((tm, tn), lambda j, i: (i, j)),
        out_shape=jax.ShapeDtypeStruct((r, f), BF16),
        compiler_params=_cparams(("parallel", "parallel")),
        name="ffn_up",
    )(x, wg, wu)


def _rope_table(positions, scale):
    half = QK_ROPE // 2
    inv = jnp.power(ROPE_BASE, -jnp.arange(half, dtype=F32) / half)
    ang = positions.astype(F32)[:, None] * inv[None, :]
    cos, sin = jnp.cos(ang), jnp.sin(ang)
    return jnp.concatenate([cos, cos, -sin, sin], axis=-1) * scale


def _rope_cols(w):
    half = QK_ROPE // 2
    x1, x2 = w[:, :half], w[:, half:]
    return jnp.concatenate([x1, x2, x2, x1], axis=1)


def _split_w_in(w, d):
    qk = QK_NOPE + QK_ROPE
    o_q = HEADS * qk
    o_ckv = o_q + KV_RANK
    o_kr = o_ckv + QK_ROPE
    o_conv = o_kr + 2 * d
    o_u = o_conv + d
    wq = w[:, :o_q].reshape(d, HEADS, qk)
    slots = [jnp.concatenate([wq[:, h, :QK_NOPE], _rope_cols(wq[:, h, QK_NOPE:])], axis=1)
             for h in range(HEADS)]
    w_q = jnp.concatenate(slots, axis=1)
    w_ckv = jnp.concatenate([w[:, o_q:o_ckv], _rope_cols(w[:, o_ckv:o_kr])], axis=1)
    w_conv = w[:, o_kr:o_conv]
    w_u = w[:, o_conv:o_u]
    w_g = w[:, o_u:]
    return tuple(a.astype(BF16) for a in (w_q, w_ckv, w_conv, w_u, w_g))


def _split_w_ukv(w):
    wr = w.reshape(KV_RANK, HEADS, QK_NOPE + V_DIM)
    w_k = wr[:, :, :QK_NOPE].reshape(KV_RANK, HEADS * QK_NOPE)
    w_vt = wr[:, :, QK_NOPE:].reshape(KV_RANK, HEADS * V_DIM).T
    return w_k.astype(BF16), w_vt.astype(BF16)


def kernel(x, meta_tokens, ln0_g, ln0_b, w_in, b_gate, kv_norm_g, w_ukv, w_mla_out, conv_w, conv_b, conv_ln_g, conv_ln_b, w_conv_out, ssm_lam_re, ssm_lam_im, ssm_log_dt, ssm_b_re, ssm_b_im, ssm_c_re, ssm_c_im, ssm_d, w_glu, w_out, ln1_g, ln1_b, w_ffn_gate, w_ffn_up, w_ffn_down, ln2_g, ln2_b):
    bsz, seq, d = x.shape
    depth = w_in.shape[0]
    rows_real = bsz * seq
    assert seq % BQ == 0 and seq % TT_CONV == 0 and META_TILE % TM_LN == 0
    alpha = (2 * depth) ** 0.25

    x_rows = jnp.concatenate([x.reshape(rows_real, d), meta_tokens.astype(x.dtype),
                              jnp.zeros((META_TILE - N_META, d), x.dtype)], axis=0)
    pos = jnp.concatenate([jnp.tile(N_META + jnp.arange(seq), bsz), jnp.arange(META_TILE)])
    q_tbl = _rope_table(pos, Q_SCALE)
    k_tbl = _rope_table(pos, 1.0)
    n_scan = (seq // SSM_CHUNK).bit_length() - 1
    prep = _ssm_prep(ssm_lam_re, ssm_lam_im, ssm_log_dt, ssm_b_re, ssm_b_im, ssm_c_re, ssm_c_im,
                     n_scan)

    h, hb = _ln0(x_rows, ln0_g, ln0_b)
    for l in range(depth):
        w_q, w_ckv, w_conv, w_u, w_g = _split_w_in(w_in[l], d)
        q = _matmul(hb, w_q, tn=4 * HEAD_SLOT, rope_tbl=q_tbl, name="in_proj_q")
        ckv = _matmul(hb, w_ckv, tn=CKV_COLS, name="in_proj_ckv")
        pc = _matmul(hb, w_conv, tn=1024, name="in_proj_conv")
        u_nat = _matmul(hb, w_u, tn=1024, out_dtype=F32, name="in_proj_u")
        gates = _matmul(hb, w_g, tn=1024, bias=b_gate[l], name="in_proj_gates")

        w_k, w_vt = _split_w_ukv(w_ukv[l])
        k, vt = _kv_up(ckv, kv_norm_g[l], w_k, w_vt, k_tbl)
        o = _attention(q, k, vt, bsz, seq)
        o_meta = _attention_meta(q, k, vt, rows_real)
        zc = _conv_branch(pc, conv_w[l], conv_b[l], conv_ln_g[l], conv_ln_b[l], rows_real, seq)
        gy, gy_meta = _ssm_branch(u_nat, prep, ssm_d[l:l + 1], l, bsz, seq)

        mix = _mix(o, o_meta, zc, gy, gy_meta, gates, w_mla_out[l].astype(BF16),
                   w_conv_out[l].astype(BF16), w_glu[l].astype(BF16))
        h, hb = _mm_res_ln(mix, w_out[l].astype(BF16), h, ln1_g[l], ln1_b[l], alpha=alpha, tk=d)
        ff = _ffn_up(hb, w_ffn_gate[l].astype(BF16), w_ffn_up[l].astype(BF16))
        h, hb = _mm_res_ln(ff, w_ffn_down[l].astype(BF16), h, ln2_g[l], ln2_b[l], alpha=alpha,
                           tk=w_ffn_down.shape[1] // 4)
    return h[:rows_real].reshape(bsz, seq, d)
```

```python
import functools
import math

import jax
import jax.numpy as jnp
from jax import lax
from jax.experimental import pallas as pl
from jax.experimental.pallas import tpu as pltpu

F32 = jnp.float32
BF16 = jnp.bfloat16

N_META = 16
CHUNK = 64
HEADS = 16
QK_NOPE = 128
QK_ROPE = 64
V_DIM = 128
KV_RANK = 512
ROPE_BASE = 10000.0
MLA_SCALE = (QK_NOPE + QK_ROPE) ** -0.5
LOG2E = math.log2(math.e)
Q_SCALE = MLA_SCALE * LOG2E
CONV_K = 31
SSM_GROUP = 16
SSM_STATE = 64
LN_EPS = 1e-5
RMS_EPS = 1e-6

LANES = 128
HEAD_SLOT = 2 * LANES
CKV_COLS = KV_RANK + LANES
META_TILE = 512
CONV_HALO = 32
VMEM_LIMIT = 56 * 1024 * 1024

TM_MM = (1280, 1024, 768, 512)
TM_LN = 512
TM_FFN_DOWN = (640, 512)
TM_KV = 512
TT_CONV = 256
BQ = 1024
ATT_HEADS = 4
ATT_SUB = 256
SSM_CHUNK = 32
SSM_GB = LANES // SSM_GROUP


def _row_tile(rows, candidates):
    return next(t for t in candidates if rows % t == 0)


def _cparams(sem):
    return pltpu.CompilerParams(dimension_semantics=sem, vmem_limit_bytes=VMEM_LIMIT)


def _dot(a, b):
    return jnp.dot(a, b, preferred_element_type=F32)


def _dot_nt(a, b):
    return lax.dot_general(a, b, (((1,), (1,)), ((), ())), preferred_element_type=F32)


def _dot_exact(a, b):
    return jnp.dot(a, b, preferred_element_type=F32, precision=lax.Precision.HIGHEST)


def _layer_norm(x, g, b):
    mu = jnp.mean(x, axis=-1, keepdims=True)
    xc = x - mu
    var = jnp.mean(xc * xc, axis=-1, keepdims=True)
    return xc * lax.rsqrt(var + LN_EPS) * g + b


def _rope_slot(a, tbl):
    p = a * tbl
    return p + pltpu.roll(p, 64, 1)


def _ln0_kernel(x_ref, xm_ref, g_ref, b_ref, h_ref, hb_ref, *, n_real):
    x = jnp.where(pl.program_id(0) == n_real, xm_ref[...], x_ref[...])
    y = _layer_norm(x, g_ref[...], b_ref[...])
    h_ref[...] = y
    hb_ref[...] = y.astype(BF16)


def _ln0(x_real, x_meta, g, b):
    rows_real, d = x_real.shape
    tm = META_TILE
    n_real = rows_real // tm
    r = rows_real + META_TILE
    return pl.pallas_call(
        functools.partial(_ln0_kernel, n_real=n_real),
        grid=(r // tm,),
        in_specs=[pl.BlockSpec((tm, d), lambda i: (jnp.minimum(i, n_real - 1), 0)),
                  pl.BlockSpec((tm, d), lambda i: (0, 0)),
                  pl.BlockSpec((1, d), lambda i: (0, 0)),
                  pl.BlockSpec((1, d), lambda i: (0, 0))],
        out_specs=[pl.BlockSpec((tm, d), lambda i: (i, 0)),
                   pl.BlockSpec((tm, d), lambda i: (i, 0))],
        out_shape=[jax.ShapeDtypeStruct((r, d), F32), jax.ShapeDtypeStruct((r, d), BF16)],
        compiler_params=_cparams(("parallel",)),
        name="ln0",
    )(x_real, x_meta, g.reshape(1, d), b.reshape(1, d))


def _mm_kernel(x_ref, w_ref, o_ref):
    o_ref[...] = _dot(x_ref[...], w_ref[...]).astype(o_ref.dtype)


def _mm_sigmoid_kernel(x_ref, w_ref, b_ref, o_ref):
    o_ref[...] = jax.nn.sigmoid(_dot(x_ref[...], w_ref[...]) + b_ref[...]).astype(o_ref.dtype)


def _mm_qrope_kernel(x_ref, w_ref, t_ref, o_ref, *, heads_per_tile):
    acc = _dot(x_ref[...], w_ref[...])
    tbl = t_ref[...]
    for hh in range(heads_per_tile):
        c0 = hh * HEAD_SLOT
        o_ref[:, c0:c0 + LANES] = (acc[:, c0:c0 + LANES] * Q_SCALE).astype(o_ref.dtype)
        o_ref[:, c0 + LANES:c0 + HEAD_SLOT] = _rope_slot(
            acc[:, c0 + LANES:c0 + HEAD_SLOT], tbl).astype(o_ref.dtype)


def _matmul(x, w, *, tn, bias=None, rope_tbl=None, out_dtype=BF16, name):
    r, k = x.shape
    n = w.shape[1]
    tm = _row_tile(r, TM_MM)
    in_specs = [pl.BlockSpec((tm, k), lambda j, i: (i, 0)),
                pl.BlockSpec((k, tn), lambda j, i: (0, j))]
    args = [x, w]
    if bias is not None:
        kern = _mm_sigmoid_kernel
        in_specs.append(pl.BlockSpec((1, tn), lambda j, i: (0, j)))
        args.append(bias.reshape(1, n))
    elif rope_tbl is not None:
        kern = functools.partial(_mm_qrope_kernel, heads_per_tile=tn // HEAD_SLOT)
        in_specs.append(pl.BlockSpec((tm, LANES), lambda j, i: (i, 0)))
        args.append(rope_tbl)
    else:
        kern = _mm_kernel
    return pl.pallas_call(
        kern,
        grid=(n // tn, r // tm),
        in_specs=in_specs,
        out_specs=pl.BlockSpec((tm, tn), lambda j, i: (i, j)),
        out_shape=jax.ShapeDtypeStruct((r, n), out_dtype),
        compiler_params=_cparams(("parallel", "parallel")),
        name=name,
    )(*args)


def _kv_up_kernel(c_ref, g_ref, wk_ref, wvt_ref, t_ref, k_ref, vt_ref):
    c = c_ref[:, :KV_RANK].astype(F32)
    ms = jnp.mean(c * c, axis=-1, keepdims=True)
    cn = (c * lax.rsqrt(ms + RMS_EPS) * g_ref[...]).astype(BF16)
    kn = _dot(cn, wk_ref[...])
    vt_ref[...] = _dot_nt(wvt_ref[...], cn).astype(BF16)
    kr = _rope_slot(c_ref[:, KV_RANK:].astype(F32), t_ref[...])
    lane = lax.broadcasted_iota(jnp.int32, kr.shape, 1)
    kr = jnp.where(lane < QK_ROPE, kr, 0.0).astype(BF16)
    for h in range(HEADS):
        k_ref[:, h * HEAD_SLOT:h * HEAD_SLOT + LANES] = kn[:, h * QK_NOPE:(h + 1) * QK_NOPE].astype(BF16)
        k_ref[:, h * HEAD_SLOT + LANES:(h + 1) * HEAD_SLOT] = kr


def _kv_up(ckv, g, w_k, w_vt, k_tbl):
    r = ckv.shape[0]
    tm = TM_KV
    nk, nv = HEADS * HEAD_SLOT, HEADS * V_DIM
    return pl.pallas_call(
        _kv_up_kernel,
        grid=(r // tm,),
        in_specs=[pl.BlockSpec((tm, CKV_COLS), lambda i: (i, 0)),
                  pl.BlockSpec((1, KV_RANK), lambda i: (0, 0)),
                  pl.BlockSpec(w_k.shape, lambda i: (0, 0)),
                  pl.BlockSpec(w_vt.shape, lambda i: (0, 0)),
                  pl.BlockSpec((tm, LANES), lambda i: (i, 0))],
        out_specs=[pl.BlockSpec((tm, nk), lambda i: (i, 0)),
                   pl.BlockSpec((nv, tm), lambda i: (0, i))],
        out_shape=[jax.ShapeDtypeStruct((r, nk), BF16), jax.ShapeDtypeStruct((nv, r), BF16)],
        compiler_params=_cparams(("parallel",)),
        name="kv_up",
    )(ckv, g.reshape(1, KV_RANK), w_k, w_vt, k_tbl)


def _attn_kernel(qi_ref, kj_ref, q_ref, k_ref, vt_ref, km_ref, vmt_ref, o_ref, m_ref, l_ref, acc_ref):
    step = pl.program_id(2)
    qi = qi_ref[step]
    kj = kj_ref[step]
    sub = ATT_SUB
    nsub = q_ref.shape[0] // sub

    def cols(j):
        return slice(j * sub, (j + 1) * sub)

    def head_cols(hh, width):
        return slice(hh * width, (hh + 1) * width)

    def load_state(hh):
        return ([m_ref[hh, :, cols(j)] for j in range(nsub)], [l_ref[hh, :, cols(j)] for j in range(nsub)],
                [acc_ref[hh, :, cols(j)] for j in range(nsub)])

    def store_state(hh, m, l, acc):
        for j in range(nsub):
            m_ref[hh, :, cols(j)] = m[j]
            l_ref[hh, :, cols(j)] = l[j]
            acc_ref[hh, :, cols(j)] = acc[j]

    def attend(key_blocks, diagonal):
        scores, maxes = [], []
        for hh in range(ATT_HEADS):
            m, _, _ = load_state(hh)
            st_h = []
            for c, (keys, _, mask) in enumerate(key_blocks):
                j0 = c if diagonal else 0
                st = _dot_nt(keys(hh), q_ref[j0 * sub:, head_cols(hh, HEAD_SLOT)])
                pieces = []
                for j in range(j0, nsub):
                    sj = st[:, (j - j0) * sub:(j - j0 + 1) * sub]
                    if mask is not None:
                        sj = mask(sj)
                    elif diagonal and j == c:
                        kc = lax.broadcasted_iota(jnp.int32, sj.shape, 0) // CHUNK
                        qc = lax.broadcasted_iota(jnp.int32, sj.shape, 1) // CHUNK
                        sj = jnp.where(kc <= qc, sj, -jnp.inf)
                    m[j] = jnp.maximum(m[j], jnp.max(sj, axis=0, keepdims=True))
                    pieces.append(sj)
                st_h.append(pieces)
            scores.append(st_h)
            maxes.append(m)
        for hh in range(ATT_HEADS):
            m_old, l, acc = load_state(hh)
            m = maxes[hh]
            alpha = [jnp.exp2(m_old[j] - m[j]) for j in range(nsub)]
            l = [alpha[j] * l[j] for j in range(nsub)]
            acc = [alpha[j] * acc[j] for j in range(nsub)]
            for c, (_, values, _) in enumerate(key_blocks):
                j0 = c if diagonal else 0
                vt = values(hh)
                for j in range(j0, nsub):
                    p = jnp.exp2(scores[hh][c][j - j0] - m[j])
                    l[j] = l[j] + jnp.sum(p, axis=0, keepdims=True)
                    acc[j] = acc[j] + _dot(vt, p.astype(BF16))
            store_state(hh, m, l, acc)

    @pl.when(kj == 0)
    def _():
        for hh in range(ATT_HEADS):
            m_ref[hh] = jnp.full(m_ref.shape[1:], -jnp.inf, F32)
            l_ref[hh] = jnp.zeros(l_ref.shape[1:], F32)
            acc_ref[hh] = jnp.zeros(acc_ref.shape[1:], F32)

        def meta_mask(st):
            key = lax.broadcasted_iota(jnp.int32, st.shape, 0)
            return jnp.where(key < N_META, st, -jnp.inf)

        attend([(lambda hh: km_ref[:, head_cols(hh, HEAD_SLOT)],
                 lambda hh: vmt_ref[head_cols(hh, V_DIM), :], meta_mask)], False)

    def key_blocks():
        return [(lambda hh, c=c: k_ref[c * sub:(c + 1) * sub, head_cols(hh, HEAD_SLOT)],
                 lambda hh, c=c: vt_ref[head_cols(hh, V_DIM), c * sub:(c + 1) * sub], None)
                for c in range(nsub)]

    @pl.when(kj < qi)
    def _():
        attend(key_blocks(), False)

    @pl.when(kj == qi)
    def _():
        attend(key_blocks(), True)
        for hh in range(ATT_HEADS):
            for j in range(nsub):
                o = acc_ref[hh, :, cols(j)] / l_ref[hh, :, cols(j)]
                o_ref[cols(j), head_cols(hh, V_DIM)] = o.T.astype(o_ref.dtype)


def _attention(q, k, vt, bsz, seq):
    r = bsz * seq
    bq = BQ
    nq = seq // bq
    nh = ATT_HEADS
    pairs = [(i, j) for i in range(nq) for j in range(i + 1)]
    qi = jnp.asarray([p[0] for p in pairs], jnp.int32)
    kj = jnp.asarray([p[1] for p in pairs], jnp.int32)
    meta_blk = (bsz * seq) // LANES
    grid_spec = pltpu.PrefetchScalarGridSpec(
        num_scalar_prefetch=2,
        grid=(bsz, HEADS // nh, len(pairs)),
        in_specs=[
            pl.BlockSpec((bq, nh * HEAD_SLOT), lambda b, h, s, qi, kj: (b * nq + qi[s], h)),
            pl.BlockSpec((bq, nh * HEAD_SLOT), lambda b, h, s, qi, kj: (b * nq + kj[s], h)),
            pl.BlockSpec((nh * V_DIM, bq), lambda b, h, s, qi, kj: (h, b * nq + kj[s])),
            pl.BlockSpec((LANES, nh * HEAD_SLOT), lambda b, h, s, qi, kj: (meta_blk, h)),
            pl.BlockSpec((nh * V_DIM, LANES), lambda b, h, s, qi, kj: (h, meta_blk)),
        ],
        out_specs=pl.BlockSpec((bq, nh * V_DIM), lambda b, h, s, qi, kj: (b * nq + qi[s], h)),
        scratch_shapes=[pltpu.VMEM((nh, 1, bq), F32), pltpu.VMEM((nh, 1, bq), F32),
                        pltpu.VMEM((nh, V_DIM, bq), F32)],
    )
    return pl.pallas_call(
        _attn_kernel,
        grid_spec=grid_spec,
        out_shape=jax.ShapeDtypeStruct((r, HEADS * V_DIM), BF16),
        compiler_params=_cparams(("parallel", "parallel", "arbitrary")),
        name="mla_attn",
    )(qi, kj, q, k, vt, k, vt)


def _attn_meta_kernel(q_ref, km_ref, vmt_ref, o_ref):
    s = _dot_nt(q_ref[...], km_ref[...])
    lane = lax.broadcasted_iota(jnp.int32, s.shape, 1)
    s = jnp.where(lane < N_META, s, -jnp.inf)
    m = jnp.max(s, axis=-1, keepdims=True)
    p = jnp.exp2(s - m)
    l = jnp.sum(p, axis=-1, keepdims=True)
    o_ref[...] = (_dot_nt(p.astype(BF16), vmt_ref[...]) / l).astype(o_ref.dtype)


def _attention_meta(q, k, vt, rows_real):
    meta_blk = rows_real // LANES
    tile_blk = rows_real // META_TILE
    return pl.pallas_call(
        _attn_meta_kernel,
        grid=(HEADS,),
        in_specs=[pl.BlockSpec((META_TILE, HEAD_SLOT), lambda h: (tile_blk, h)),
                  pl.BlockSpec((LANES, HEAD_SLOT), lambda h: (meta_blk, h)),
                  pl.BlockSpec((V_DIM, LANES), lambda h: (h, meta_blk))],
        out_specs=pl.BlockSpec((META_TILE, V_DIM), lambda h: (0, h)),
        out_shape=jax.ShapeDtypeStruct((META_TILE, HEADS * V_DIM), BF16),
        compiler_params=_cparams(("parallel",)),
        name="mla_attn_meta",
    )(q, k, vt)


CONV_RB = 64
CONV_LB = 256


def _conv_kernel(val_ref, gate_ref, hval_ref, hgate_ref, mval_ref, mgate_ref, w_ref, cb_ref,
                 g_ref, b_ref, o_ref, z_ref, c_ref, *, tt, tiles_per_seq, n_real_tiles):
    i = pl.program_id(0)
    chans = o_ref.shape[1]

    def glu(v, g):
        return v.astype(F32) * jax.nn.sigmoid(g.astype(F32))

    z_ref[CONV_HALO:CONV_HALO + tt, :] = glu(val_ref[...], gate_ref[...])
    z_ref[CONV_HALO + tt:, :] = jnp.zeros((8, chans), F32)
    is_meta = i >= n_real_tiles
    first = jnp.logical_and(i % tiles_per_seq == 0, jnp.logical_not(is_meta))
    z_prev = glu(hval_ref[...], hgate_ref[...])
    z_meta = glu(mval_ref[...], mgate_ref[...])
    pad = CONV_HALO - N_META
    z_start = jnp.concatenate([jnp.zeros((pad, chans), F32), z_meta[:N_META]], axis=0)
    z_ref[:CONV_HALO, :] = jnp.where(is_meta, 0.0, jnp.where(first, z_start, z_prev))

    off = CONV_HALO - (CONV_K - 1)
    for rb in range(tt // CONV_RB):
        for lb in range(chans // CONV_LB):
            cs = slice(lb * CONV_LB, (lb + 1) * CONV_LB)
            acc = jnp.zeros((CONV_RB, CONV_LB), F32)
            for res in range(8):
                taps = [k for k in range(CONV_K) if (off + k) % 8 == res]
                base = rb * CONV_RB + (off + taps[0]) - res
                part = jnp.zeros((CONV_RB + 8, CONV_LB), F32)
                for k in taps:
                    r0 = base + (k - taps[0])
                    wk = jnp.tile(w_ref[8 * k:8 * k + 8, cs], (CONV_RB // 8 + 1, 1))
                    part = part + z_ref[r0:r0 + CONV_RB + 8, cs] * wk
                acc = acc + part[res:res + CONV_RB]
            c_ref[rb * CONV_RB:(rb + 1) * CONV_RB, cs] = acc + cb_ref[:, cs]
    y = _layer_norm(c_ref[...], g_ref[...], b_ref[...])
    o_ref[...] = (y * jax.nn.sigmoid(y)).astype(o_ref.dtype)


def _conv_branch(pc, conv_w, conv_b, ln_g, ln_b, rows_real, seq):
    r = pc.shape[0]
    chans = conv_w.shape[1]
    tt = TT_CONV
    hb = tt // CONV_HALO
    kern = functools.partial(_conv_kernel, tt=tt, tiles_per_seq=seq // tt,
                             n_real_tiles=rows_real // tt)
    meta_blk = rows_real // CONV_HALO

    def halo(i):
        return jnp.maximum(i * hb - 1, 0)

    row = lambda a: a.reshape(1, chans)
    return pl.pallas_call(
        kern,
        grid=(r // tt,),
        in_specs=[pl.BlockSpec((tt, chans), lambda i: (i, 0)),
                  pl.BlockSpec((tt, chans), lambda i: (i, 1)),
                  pl.BlockSpec((CONV_HALO, chans), lambda i: (halo(i), 0)),
                  pl.BlockSpec((CONV_HALO, chans), lambda i: (halo(i), 1)),
                  pl.BlockSpec((CONV_HALO, chans), lambda i: (meta_blk, 0)),
                  pl.BlockSpec((CONV_HALO, chans), lambda i: (meta_blk, 1)),
                  pl.BlockSpec((8 * CONV_K, chans), lambda i: (0, 0)),
                  pl.BlockSpec((1, chans), lambda i: (0, 0)),
                  pl.BlockSpec((1, chans), lambda i: (0, 0)),
                  pl.BlockSpec((1, chans), lambda i: (0, 0))],
        out_specs=pl.BlockSpec((tt, chans), lambda i: (i, 0)),
        out_shape=jax.ShapeDtypeStruct((r, chans), BF16),
        scratch_shapes=[pltpu.VMEM((CONV_HALO + tt + 8, chans), F32), pltpu.VMEM((tt, chans), F32)],
        compiler_params=_cparams(("parallel",)),
        name="conv_branch",
    )(pc, pc, pc, pc, pc, pc, jnp.repeat(conv_w, 8, axis=0), row(conv_b), row(ln_g), row(ln_b))


SSM_SL = SSM_CHUNK * SSM_GROUP


def _ssm_prep_kernel(lc_ref, lr_ref, b_ref, c_ref, cc_ref, e_ref, et_ref, es_ref, est_ref, tt_ref,
                     pt_ref, qst_ref, sc_ref, *, n_steps):
    ns, ch = SSM_STATE, SSM_CHUNK

    def power(lre_dt, ang, e):
        mag = jnp.exp(lre_dt * e)
        return mag * jnp.cos(ang * e), mag * jnp.sin(ang * e)

    lc = lc_ref[0]
    lam_re, lam_im = lc[:, 0:1], lc[:, 1:2]
    dt = jnp.exp(lc[:, 2:3])
    lre_dt, ang = lam_re * dt, lam_im * dt
    a_re, a_im = power(lre_dt, ang, 1.0)
    den = lam_re * lam_re + lam_im * lam_im
    f_re = (((a_re - 1.0) * lam_re + a_im * lam_im) / den)[:ns]
    f_im = ((a_im * lam_re - (a_re - 1.0) * lam_im) / den)[:ns]
    lane1 = lax.broadcasted_iota(jnp.int32, (1, LANES), 1)
    back = jnp.maximum(ch - 1 - lane1, 0).astype(F32)
    ac_re, ac_im = power(lre_dt[:ns], ang[:ns], back)
    ae_re = _dot_exact(ac_re, es_ref[...])
    ae_im = _dot_exact(ac_im, es_ref[...])
    expand = e_ref[...]
    b_re = _dot_exact(b_ref[0, 0], expand)
    b_im = _dot_exact(b_ref[0, 1], expand)
    bb_re = f_re * b_re - f_im * b_im
    bb_im = f_re * b_im + f_im * b_re
    p_re = ae_re * bb_re - ae_im * bb_im
    p_im = ae_re * bb_im + ae_im * bb_re
    pt_ref[0, :ns, :] = p_re.astype(pt_ref.dtype)
    pt_ref[0, ns:, :] = p_im.astype(pt_ref.dtype)
    rev = _dot_exact(c_ref[0, 0], p_re) - _dot_exact(c_ref[0, 1], p_im)
    lane_t = lax.broadcasted_iota(jnp.int32, rev.shape, 1)
    for t in range(ch):
        w = (t + 1) * SSM_GROUP
        rolled = rev if w == SSM_SL else pltpu.roll(rev, w, 1)
        tt_ref[0, t * SSM_GROUP:(t + 1) * SSM_GROUP, :] = jnp.where(
            lane_t < w, rolled, 0.0).astype(tt_ref.dtype)
    lane2 = lax.broadcasted_iota(jnp.int32, (LANES, LANES), 1)
    row2 = lax.broadcasted_iota(jnp.int32, (LANES, LANES), 0)
    expo = (ch * jnp.left_shift(1, jnp.minimum(lane2[0:1] // 2, n_steps))).astype(F32)
    s_re, s_im = power(lre_dt, ang, expo)
    sc_ref[0] = jnp.where(lane2 % 2 == 0, s_re, jnp.where(row2 < ns, -s_im, s_im))

    lr = lr_ref[0]
    dt_r = jnp.exp(lr[2:3])
    tp1 = (lax.broadcasted_iota(jnp.int32, (ch, 1), 0) + 1).astype(F32)
    at_re, at_im = power(lr[0:1] * dt_r, lr[1:2] * dt_r, tp1)
    ap_re = _dot_exact(est_ref[...], at_re)
    ap_im = _dot_exact(est_ref[...], at_im)
    ctile = _dot_exact(et_ref[...], cc_ref[0])
    sign = jnp.where(lane2[0:1] < ns, 1.0, -1.0)
    qst_ref[0] = (sign * (ap_re * ctile) - pltpu.roll(ap_im * ctile, ns, 1)).astype(qst_ref.dtype)


def _ssm_prep(lam_re, lam_im, log_dt, b_re, b_im, c_re, c_im, n_steps):
    nl, ng, ns = lam_re.shape
    lg = nl * ng
    dtb = jnp.broadcast_to(log_dt[..., None], lam_re.shape)
    lrow = jnp.stack([lam_re, lam_im, dtb], axis=-2)
    lrow = jnp.concatenate([lrow, lrow], axis=-1).reshape(lg, 3, 2 * ns)
    lcol = jnp.swapaxes(lrow, -1, -2)
    b = jnp.stack([b_re, b_im], axis=2).reshape(lg, 2, ns, SSM_GROUP)
    c = jnp.stack([c_re, c_im], axis=2).reshape(lg, 2, SSM_GROUP, ns)
    ccat = jnp.concatenate([c_re, c_im], axis=-1).reshape(lg, SSM_GROUP, 2 * ns)
    lane_c = jnp.arange(SSM_SL) % SSM_GROUP
    expand = (lane_c[None, :] == jnp.arange(SSM_GROUP)[:, None]).astype(F32)
    lane_s = jnp.arange(SSM_SL) // SSM_GROUP
    expand_step = (lane_s[None, :] == jnp.arange(LANES)[:, None]).astype(F32)
    expand_step_t = (lane_s[:, None] == jnp.arange(SSM_CHUNK)[None, :]).astype(F32)
    const = lambda a: pl.BlockSpec(a.shape, lambda i: (0, 0))
    blk = lambda *shape: pl.BlockSpec((1,) + shape, lambda i: (i,) + (0,) * len(shape))
    return pl.pallas_call(
        functools.partial(_ssm_prep_kernel, n_steps=n_steps),
        grid=(lg,),
        in_specs=[blk(2 * ns, 3), blk(3, 2 * ns), blk(2, ns, SSM_GROUP), blk(2, SSM_GROUP, ns),
                  blk(SSM_GROUP, 2 * ns),
                  const(expand), const(expand.T), const(expand_step), const(expand_step_t)],
        out_specs=[blk(SSM_SL, SSM_SL), blk(2 * ns, SSM_SL), blk(SSM_SL, 2 * ns), blk(2 * ns, 2 * ns)],
        out_shape=[jax.ShapeDtypeStruct((lg, SSM_SL, SSM_SL), BF16),
                   jax.ShapeDtypeStruct((lg, 2 * ns, SSM_SL), BF16),
                   jax.ShapeDtypeStruct((lg, SSM_SL, 2 * ns), BF16),
                   jax.ShapeDtypeStruct((lg, 2 * ns, 2 * ns), F32)],
        compiler_params=_cparams(("parallel",)),
        name="ssm_prep",
    )(lcol, lrow, b, c, ccat, expand, expand.T, expand_step, expand_step_t)


def _swap_halves(x):
    return pltpu.roll(x, SSM_STATE, 0)


def _ssm_kernel(u_ref, tt_ref, pt_ref, qst_ref, sc_ref, d_ref, um_ref, o_ref, ym_ref, ut_ref, yt_ref,
                us_ref, *, nk, n_steps):
    ch = SSM_CHUNK
    for s in range(ch):
        us = u_ref[pl.ds(s, nk, stride=ch), :]
        us_ref[s] = us
        ust = us.T
        for g in range(SSM_GB):
            ut_ref[g, s * SSM_GROUP:(s + 1) * SSM_GROUP, :] = (
                ust[g * SSM_GROUP:(g + 1) * SSM_GROUP, :].astype(BF16))
    lane = lax.broadcasted_iota(jnp.int32, (LANES, nk), 1)
    for g in range(SSM_GB):
        ut = ut_ref[g]
        um = um_ref[g]
        tt, pt, sc = tt_ref[g], pt_ref[g], sc_ref[g]
        ym_ref[g] = _dot_nt(um, tt)[:8]
        g0 = _dot_nt(pt, um)[:, 0:1]
        seed = sc[:, 0:1] * g0 + sc[:, 1:2] * _swap_halves(g0)
        e = _dot(pt, ut) + jnp.where(lane == 0, seed, 0.0)
        for i in range(n_steps):
            sh = jnp.where(lane >= 2 ** i, pltpu.roll(e, 2 ** i, 1), 0.0)
            e = e + sc[:, 2 * i:2 * i + 1] * sh + sc[:, 2 * i + 1:2 * i + 2] * _swap_halves(sh)
        state_in = jnp.where(lane == 0, g0, pltpu.roll(e, 1, 1))
        yt_ref[g] = _dot(tt, ut) + _dot(qst_ref[g], state_in.astype(BF16))
    for t in range(ch):
        yt = jnp.concatenate([yt_ref[g, t * SSM_GROUP:(t + 1) * SSM_GROUP, :]
                              for g in range(SSM_GB)], axis=0)
        y = yt.T + d_ref[...] * us_ref[t]
        o_ref[pl.ds(t, nk, stride=ch), :] = jax.nn.gelu(y, approximate=True)


def _ssm_meta_kernel(ym_ref, um_ref, d_ref, o_ref):
    o_ref[N_META:, :] = jnp.zeros((o_ref.shape[0] - N_META, o_ref.shape[1]), o_ref.dtype)
    y = ym_ref[...] + d_ref[...] * um_ref[...]
    o_ref[:N_META, :] = jax.nn.gelu(y, approximate=True)


def _ssm_branch(u_nat, prep, d_row, layer, bsz, seq):
    tt_all, pt_all, qst_all, sc_all = prep
    chans = u_nat.shape[1]
    ng = chans // SSM_GROUP
    nj = ng // SSM_GB
    rows_real = bsz * seq
    ch = SSM_CHUNK
    nk = seq // ch
    n_steps = nk.bit_length() - 1
    assert nk == 2 ** n_steps and nk % LANES == 0 and ch >= N_META
    u_meta = u_nat[rows_real:rows_real + N_META]
    um = u_meta.reshape(N_META, ng, SSM_GROUP).transpose(1, 0, 2).reshape(ng, N_META * SSM_GROUP)
    um = jnp.pad(um, ((0, 0), ((ch - N_META) * SSM_GROUP, 0)))
    um = jnp.pad(um[:, None, :], ((0, 0), (0, LANES - 1), (0, 0))).astype(BF16)
    j0 = layer * nj
    blk = lambda *shape: pl.BlockSpec((SSM_GB,) + shape, lambda j, b: (j0 + j,) + (0,) * len(shape))
    kern = functools.partial(_ssm_kernel, nk=nk, n_steps=n_steps)
    gy, ym = pl.pallas_call(
        kern,
        grid=(nj, bsz),
        in_specs=[pl.BlockSpec((seq, LANES), lambda j, b: (b, j)),
                  blk(SSM_SL, SSM_SL), blk(LANES, SSM_SL), blk(SSM_SL, LANES), blk(LANES, LANES),
                  pl.BlockSpec((1, LANES), lambda j, b: (0, j)),
                  pl.BlockSpec((SSM_GB, LANES, SSM_SL), lambda j, b: (j, 0, 0))],
        out_specs=[pl.BlockSpec((seq, LANES), lambda j, b: (b, j)),
                   pl.BlockSpec((SSM_GB, 8, SSM_SL), lambda j, b: (j, 0, 0))],
        out_shape=[jax.ShapeDtypeStruct((rows_real, chans), F32),
                   jax.ShapeDtypeStruct((ng, 8, SSM_SL), F32)],
        scratch_shapes=[pltpu.VMEM((SSM_GB, SSM_SL, nk), BF16), pltpu.VMEM((SSM_GB, SSM_SL, nk), F32),
                        pltpu.VMEM((ch, nk, LANES), F32)],
        compiler_params=_cparams(("parallel", "arbitrary")),
        name="ssm_scan",
    )(u_nat, tt_all, pt_all, qst_all, sc_all, d_row, um)
    y_meta = ym[:, 0, :].reshape(ng, ch, SSM_GROUP)[:, ch - N_META:, :]
    y_meta = y_meta.transpose(1, 0, 2).reshape(N_META, chans)
    gy_meta = pl.pallas_call(
        _ssm_meta_kernel,
        grid=(1,),
        in_specs=[pl.BlockSpec((N_META, chans), lambda i: (0, 0)),
                  pl.BlockSpec((N_META, chans), lambda i: (0, 0)),
                  pl.BlockSpec((1, chans), lambda i: (0, 0))],
        out_specs=pl.BlockSpec((META_TILE, chans), lambda i: (0, 0)),
        out_shape=jax.ShapeDtypeStruct((META_TILE, chans), F32),
        compiler_params=_cparams(("arbitrary",)),
        name="ssm_meta",
    )(y_meta, u_meta, d_row)
    return gy, gy_meta


def _mix_kernel(o_ref, om_ref, zc_ref, gy_ref, gym_ref, ga_ref, gb_ref, gc_ref, wa_ref, wb_ref,
                wza_ref, wzb_ref, out_ref, *, n_real):
    is_meta = pl.program_id(1) == n_real
    ya = _dot(jnp.where(is_meta, om_ref[...], o_ref[...]), wa_ref[...])
    yb = _dot(zc_ref[...], wb_ref[...])
    gy = jnp.where(is_meta, gym_ref[...], gy_ref[...]).astype(BF16)
    yc = _dot(gy, wza_ref[...]) * jax.nn.sigmoid(_dot(gy, wzb_ref[...]))
    mix = (ga_ref[...].astype(F32) * ya + gb_ref[...].astype(F32) * yb
           + gc_ref[...].astype(F32) * yc)
    out_ref[...] = mix.astype(out_ref.dtype)


def _mix(o, o_meta, zc, gy, gy_meta, gates, w_mla_out, w_conv_out, w_glu):
    r, d = zc.shape
    tm, tn = META_TILE, 512
    n_real = o.shape[0] // tm
    assert o.shape[0] % tm == 0 and r == o.shape[0] + META_TILE
    nb = d // tn
    act = lambda: pl.BlockSpec((tm, d), lambda j, i: (i, 0))
    real = lambda: pl.BlockSpec((tm, d), lambda j, i: (jnp.minimum(i, n_real - 1), 0))
    meta = lambda: pl.BlockSpec((tm, d), lambda j, i: (0, 0))
    gate = lambda k: pl.BlockSpec((tm, tn), lambda j, i: (i, k * nb + j))
    wgt = lambda k: pl.BlockSpec((d, tn), lambda j, i: (0, k * nb + j))
    return pl.pallas_call(
        functools.partial(_mix_kernel, n_real=n_real),
        grid=(nb, r // tm),
        in_specs=[real(), meta(), act(), real(), meta(), gate(0), gate(1), gate(2),
                  wgt(0), wgt(0), wgt(0), wgt(1)],
        out_specs=pl.BlockSpec((tm, tn), lambda j, i: (i, j)),
        out_shape=jax.ShapeDtypeStruct((r, d), BF16),
        compiler_params=_cparams(("parallel", "parallel")),
        name="gated_mix",
    )(o, o_meta, zc, gy, gy_meta, gates, gates, gates, w_mla_out, w_conv_out, w_glu, w_glu)


def _mm_res_ln_kernel(x_ref, w_ref, r_ref, g_ref, b_ref, h_ref, hb_ref, acc_ref, *, alpha, nk):
    kk = pl.program_id(1)

    @pl.when(kk == 0)
    def _():
        acc_ref[...] = jnp.zeros_like(acc_ref)

    acc_ref[...] += _dot(x_ref[...], w_ref[...])

    @pl.when(kk == nk - 1)
    def _():
        y = _layer_norm(alpha * r_ref[...] + acc_ref[...], g_ref[...], b_ref[...])
        h_ref[...] = y
        hb_ref[...] = y.astype(BF16)


def _mm_res_ln(x, w, res, g, b, *, alpha, tk, tm_candidates=(TM_LN,)):
    r, k = x.shape
    d = w.shape[1]
    tm = _row_tile(r, tm_candidates)
    nk = k // tk
    kern = functools.partial(_mm_res_ln_kernel, alpha=alpha, nk=nk)
    return pl.pallas_call(
        kern,
        grid=(r // tm, nk),
        in_specs=[pl.BlockSpec((tm, tk), lambda i, kk: (i, kk)),
                  pl.BlockSpec((tk, d), lambda i, kk: (kk, 0)),
                  pl.BlockSpec((tm, d), lambda i, kk: (i, 0)),
                  pl.BlockSpec((1, d), lambda i, kk: (0, 0)),
                  pl.BlockSpec((1, d), lambda i, kk: (0, 0))],
        out_specs=[pl.BlockSpec((tm, d), lambda i, kk: (i, 0)),
                   pl.BlockSpec((tm, d), lambda i, kk: (i, 0))],
        out_shape=[jax.ShapeDtypeStruct((r, d), F32), jax.ShapeDtypeStruct((r, d), BF16)],
        scratch_shapes=[pltpu.VMEM((tm, d), F32)],
        compiler_params=_cparams(("parallel", "arbitrary")),
        name="mm_res_ln",
    )(x, w, res, g.reshape(1, d), b.reshape(1, d))


def _ffn_up_kernel(x_ref, wg_ref, wu_ref, o_ref):
    x = x_ref[...]
    a = _dot(x, wg_ref[...])
    o_ref[...] = (a * jax.nn.sigmoid(a) * _dot(x, wu_ref[...])).astype(o_ref.dtype)


def _ffn_up(x, wg, wu):
    r, d = x.shape
    f = wg.shape[1]
    tm, tn = _row_tile(r, TM_MM), 512
    return pl.pallas_call(
        _ffn_up_kernel,
        grid=(f // tn, r // tm),
        in_specs=[pl.BlockSpec((tm, d), lambda j, i: (i, 0)),
                  pl.BlockSpec((d, tn), lambda j, i: (0, j)),
                  pl.BlockSpec((d, tn), lambda j, i: (0, j))],
        out_specs=pl.BlockSpec((tm, tn), lambda j, i: (i, j)),
        out_shape=jax.ShapeDtypeStruct((r, f), BF16),
        compiler_params=_cparams(("parallel", "parallel")),
        name="ffn_up",
    )(x, wg, wu)


def _rope_table(positions, scale):
    half = QK_ROPE // 2
    inv = jnp.power(ROPE_BASE, -jnp.arange(half, dtype=F32) / half)
    ang = positions.astype(F32)[:, None] * inv[None, :]
    cos, sin = jnp.cos(ang), jnp.sin(ang)
    return jnp.concatenate([cos, cos, -sin, sin], axis=-1) * scale


def _rope_cols(w):
    half = QK_ROPE // 2
    x1, x2 = w[:, :half], w[:, half:]
    return jnp.concatenate([x1, x2, x2, x1], axis=1)


def _split_w_in(w, d):
    qk = QK_NOPE + QK_ROPE
    o_q = HEADS * qk
    o_ckv = o_q + KV_RANK
    o_kr = o_ckv + QK_ROPE
    o_conv = o_kr + 2 * d
    o_u = o_conv + d
    wq = w[:, :o_q].reshape(d, HEADS, qk)
    half = QK_ROPE // 2
    x1, x2 = wq[:, :, QK_NOPE:QK_NOPE + half], wq[:, :, QK_NOPE + half:]
    w_q = jnp.concatenate([wq[:, :, :QK_NOPE], x1, x2, x2, x1], axis=2).reshape(d, HEADS * HEAD_SLOT)
    w_ckv = jnp.concatenate([w[:, o_q:o_ckv], _rope_cols(w[:, o_ckv:o_kr])], axis=1)
    w_conv = w[:, o_kr:o_conv]
    w_u = w[:, o_conv:o_u]
    w_g = w[:, o_u:]
    return tuple(a.astype(BF16) for a in (w_q, w_ckv, w_conv, w_u, w_g))


def _split_w_ukv(w):
    wr = w.reshape(KV_RANK, HEADS, QK_NOPE + V_DIM)
    w_k = wr[:, :, :QK_NOPE].reshape(KV_RANK, HEADS * QK_NOPE)
    w_vt = wr[:, :, QK_NOPE:].reshape(KV_RANK, HEADS * V_DIM).T
    return w_k.astype(BF16), w_vt.astype(BF16)


def kernel(x, meta_tokens, ln0_g, ln0_b, w_in, b_gate, kv_norm_g, w_ukv, w_mla_out, conv_w, conv_b, conv_ln_g, conv_ln_b, w_conv_out, ssm_lam_re, ssm_lam_im, ssm_log_dt, ssm_b_re, ssm_b_im, ssm_c_re, ssm_c_im, ssm_d, w_glu, w_out, ln1_g, ln1_b, w_ffn_gate, w_ffn_up, w_ffn_down, ln2_g, ln2_b):
    bsz, seq, d = x.shape
    depth = w_in.shape[0]
    rows_real = bsz * seq
    assert seq % BQ == 0 and seq % TT_CONV == 0 and META_TILE % TM_LN == 0
    alpha = (2 * depth) ** 0.25

    x_meta = jnp.concatenate([meta_tokens.astype(x.dtype),
                              jnp.zeros((META_TILE - N_META, d), x.dtype)], axis=0)
    pos = jnp.concatenate([jnp.tile(N_META + jnp.arange(seq), bsz), jnp.arange(META_TILE)])
    q_tbl = _rope_table(pos, Q_SCALE)
    k_tbl = _rope_table(pos, 1.0)
    n_scan = (seq // SSM_CHUNK).bit_length() - 1
    prep = _ssm_prep(ssm_lam_re, ssm_lam_im, ssm_log_dt, ssm_b_re, ssm_b_im, ssm_c_re, ssm_c_im,
                     n_scan)

    h, hb = _ln0(x.reshape(rows_real, d), x_meta, ln0_g, ln0_b)
    for l in range(depth):
        w_q, w_ckv, w_conv, w_u, w_g = _split_w_in(w_in[l], d)
        q = _matmul(hb, w_q, tn=4 * HEAD_SLOT, rope_tbl=q_tbl, name="in_proj_q")
        ckv = _matmul(hb, w_ckv, tn=CKV_COLS, name="in_proj_ckv")
        pc = _matmul(hb, w_conv, tn=1024, name="in_proj_conv")
        u_nat = _matmul(hb, w_u, tn=1024, out_dtype=F32, name="in_proj_u")
        gates = _matmul(hb, w_g, tn=1024, bias=b_gate[l], name="in_proj_gates")

        w_k, w_vt = _split_w_ukv(w_ukv[l])
        k, vt = _kv_up(ckv, kv_norm_g[l], w_k, w_vt, k_tbl)
        o = _attention(q, k, vt, bsz, seq)
        o_meta = _attention_meta(q, k, vt, rows_real)
        zc = _conv_branch(pc, conv_w[l], conv_b[l], conv_ln_g[l], conv_ln_b[l], rows_real, seq)
        gy, gy_meta = _ssm_branch(u_nat, prep, ssm_d[l:l + 1], l, bsz, seq)

        mix = _mix(o, o_meta, zc, gy, gy_meta, gates, w_mla_out[l].astype(BF16),
                   w_conv_out[l].astype(BF16), w_glu[l].astype(BF16))
        h, hb = _mm_res_ln(mix, w_out[l].astype(BF16), h, ln1_g[l], ln1_b[l], alpha=alpha, tk=d)
        ff = _ffn_up(hb, w_ffn_gate[l].astype(BF16), w_ffn_up[l].astype(BF16))
        h, hb = _mm_res_ln(ff, w_ffn_down[l].astype(BF16), h, ln2_g[l], ln2_b[l], alpha=alpha,
                           tk=w_ffn_down.shape[1] // 4, tm_candidates=TM_FFN_DOWN)
    return h[:rows_real].reshape(bsz, seq, d)
```

```python
import functools
import math

import jax
import jax.numpy as jnp
from jax import lax
from jax.experimental import pallas as pl
from jax.experimental.pallas import tpu as pltpu

F32 = jnp.float32
BF16 = jnp.bfloat16

N_META = 16
CHUNK = 64
HEADS = 16
QK_NOPE = 128
QK_ROPE = 64
V_DIM = 128
KV_RANK = 512
ROPE_BASE = 10000.0
MLA_SCALE = (QK_NOPE + QK_ROPE) ** -0.5
LOG2E = math.log2(math.e)
Q_SCALE = MLA_SCALE * LOG2E
CONV_K = 31
SSM_GROUP = 16
SSM_STATE = 64
LN_EPS = 1e-5
RMS_EPS = 1e-6

LANES = 128
HEAD_SLOT = 2 * LANES
CKV_COLS = KV_RANK + LANES
META_TILE = 512
CONV_HALO = 32
VMEM_LIMIT = 56 * 1024 * 1024

TM_MM = (1280, 1024, 768, 512)
TM_LN = 512
TM_FFN_DOWN = (640, 512)
TM_KV = 512
TT_CONV = 256
BQ = 1024
ATT_HEADS = 4
ATT_SUB = 256
SSM_CHUNK = 32
SSM_GB = LANES // SSM_GROUP


def _row_tile(rows, candidates):
    return next(t for t in candidates if rows % t == 0)


def _cparams(sem):
    return pltpu.CompilerParams(dimension_semantics=sem, vmem_limit_bytes=VMEM_LIMIT)


def _dot(a, b):
    return jnp.dot(a, b, preferred_element_type=F32)


def _dot_nt(a, b):
    return lax.dot_general(a, b, (((1,), (1,)), ((), ())), preferred_element_type=F32)


def _dot_exact(a, b):
    return jnp.dot(a, b, preferred_element_type=F32, precision=lax.Precision.HIGHEST)


def _layer_norm(x, g, b):
    mu = jnp.mean(x, axis=-1, keepdims=True)
    xc = x - mu
    var = jnp.mean(xc * xc, axis=-1, keepdims=True)
    return xc * lax.rsqrt(var + LN_EPS) * g + b


def _rope_slot(a, tbl):
    p = a * tbl
    return p + pltpu.roll(p, 64, 1)


def _ln0_kernel(x_ref, xm_ref, g_ref, b_ref, h_ref, hb_ref, *, n_real):
    x = jnp.where(pl.program_id(0) == n_real, xm_ref[...], x_ref[...])
    y = _layer_norm(x, g_ref[...], b_ref[...])
    h_ref[...] = y
    hb_ref[...] = y.astype(BF16)


def _ln0(x_real, x_meta, g, b):
    rows_real, d = x_real.shape
    tm = META_TILE
    n_real = rows_real // tm
    r = rows_real + META_TILE
    return pl.pallas_call(
        functools.partial(_ln0_kernel, n_real=n_real),
        grid=(r // tm,),
        in_specs=[pl.BlockSpec((tm, d), lambda i: (jnp.minimum(i, n_real - 1), 0)),
                  pl.BlockSpec((tm, d), lambda i: (0, 0)),
                  pl.BlockSpec((1, d), lambda i: (0, 0)),
                  pl.BlockSpec((1, d), lambda i: (0, 0))],
        out_specs=[pl.BlockSpec((tm, d), lambda i: (i, 0)),
                   pl.BlockSpec((tm, d), lambda i: (i, 0))],
        out_shape=[jax.ShapeDtypeStruct((r, d), F32), jax.ShapeDtypeStruct((r, d), BF16)],
        compiler_params=_cparams(("parallel",)),
        name="ln0",
    )(x_real, x_meta, g.reshape(1, d), b.reshape(1, d))


def _mm_kernel(x_ref, w_ref, o_ref):
    o_ref[...] = _dot(x_ref[...], w_ref[...]).astype(o_ref.dtype)


def _mm_sigmoid_kernel(x_ref, w_ref, b_ref, o_ref):
    o_ref[...] = jax.nn.sigmoid(_dot(x_ref[...], w_ref[...]) + b_ref[...]).astype(o_ref.dtype)


def _mm_qrope_kernel(x_ref, w_ref, t_ref, o_ref, *, heads_per_tile):
    acc = _dot(x_ref[...], w_ref[...])
    tbl = t_ref[...]
    for hh in range(heads_per_tile):
        c0 = hh * HEAD_SLOT
        o_ref[:, c0:c0 + LANES] = (acc[:, c0:c0 + LANES] * Q_SCALE).astype(o_ref.dtype)
        o_ref[:, c0 + LANES:c0 + HEAD_SLOT] = _rope_slot(
            acc[:, c0 + LANES:c0 + HEAD_SLOT], tbl).astype(o_ref.dtype)


def _matmul(x, w, *, tn, bias=None, rope_tbl=None, out_dtype=BF16, name):
    r, k = x.shape
    n = w.shape[1]
    tm = _row_tile(r, TM_MM)
    in_specs = [pl.BlockSpec((tm, k), lambda j, i: (i, 0)),
                pl.BlockSpec((k, tn), lambda j, i: (0, j))]
    args = [x, w]
    if bias is not None:
        kern = _mm_sigmoid_kernel
        in_specs.append(pl.BlockSpec((1, tn), lambda j, i: (0, j)))
        args.append(bias.reshape(1, n))
    elif rope_tbl is not None:
        kern = functools.partial(_mm_qrope_kernel, heads_per_tile=tn // HEAD_SLOT)
        in_specs.append(pl.BlockSpec((tm, LANES), lambda j, i: (i, 0)))
        args.append(rope_tbl)
    else:
        kern = _mm_kernel
    return pl.pallas_call(
        kern,
        grid=(n // tn, r // tm),
        in_specs=in_specs,
        out_specs=pl.BlockSpec((tm, tn), lambda j, i: (i, j)),
        out_shape=jax.ShapeDtypeStruct((r, n), out_dtype),
        compiler_params=_cparams(("parallel", "parallel")),
        name=name,
    )(*args)


def _kv_up_kernel(c_ref, g_ref, wk_ref, wvt_ref, t_ref, k_ref, vt_ref):
    c = c_ref[:, :KV_RANK].astype(F32)
    ms = jnp.mean(c * c, axis=-1, keepdims=True)
    cn = (c * lax.rsqrt(ms + RMS_EPS) * g_ref[...]).astype(BF16)
    kn = _dot(cn, wk_ref[...])
    vt_ref[...] = _dot_nt(wvt_ref[...], cn).astype(BF16)
    kr = _rope_slot(c_ref[:, KV_RANK:].astype(F32), t_ref[...])
    lane = lax.broadcasted_iota(jnp.int32, kr.shape, 1)
    kr = jnp.where(lane < QK_ROPE, kr, 0.0).astype(BF16)
    for h in range(HEADS):
        k_ref[:, h * HEAD_SLOT:h * HEAD_SLOT + LANES] = kn[:, h * QK_NOPE:(h + 1) * QK_NOPE].astype(BF16)
        k_ref[:, h * HEAD_SLOT + LANES:(h + 1) * HEAD_SLOT] = kr


def _kv_up(ckv, g, w_k, w_vt, k_tbl):
    r = ckv.shape[0]
    tm = TM_KV
    nk, nv = HEADS * HEAD_SLOT, HEADS * V_DIM
    return pl.pallas_call(
        _kv_up_kernel,
        grid=(r // tm,),
        in_specs=[pl.BlockSpec((tm, CKV_COLS), lambda i: (i, 0)),
                  pl.BlockSpec((1, KV_RANK), lambda i: (0, 0)),
                  pl.BlockSpec(w_k.shape, lambda i: (0, 0)),
                  pl.BlockSpec(w_vt.shape, lambda i: (0, 0)),
                  pl.BlockSpec((tm, LANES), lambda i: (i, 0))],
        out_specs=[pl.BlockSpec((tm, nk), lambda i: (i, 0)),
                   pl.BlockSpec((nv, tm), lambda i: (0, i))],
        out_shape=[jax.ShapeDtypeStruct((r, nk), BF16), jax.ShapeDtypeStruct((nv, r), BF16)],
        compiler_params=_cparams(("parallel",)),
        name="kv_up",
    )(ckv, g.reshape(1, KV_RANK), w_k, w_vt, k_tbl)


def _attn_kernel(qi_ref, kj_ref, q_ref, k_ref, vt_ref, km_ref, vmt_ref, o_ref, m_ref, l_ref, acc_ref,
                 s_ref, mx_ref):
    step = pl.program_id(2)
    qi = qi_ref[step]
    kj = kj_ref[step]
    sub = ATT_SUB
    nsub = q_ref.shape[0] // sub

    def cols(j):
        return slice(j * sub, (j + 1) * sub)

    def head_cols(hh, width):
        return slice(hh * width, (hh + 1) * width)

    def load_state(hh):
        return ([m_ref[hh, :, cols(j)] for j in range(nsub)], [l_ref[hh, :, cols(j)] for j in range(nsub)],
                [acc_ref[hh, :, cols(j)] for j in range(nsub)])

    def store_state(hh, m, l, acc):
        for j in range(nsub):
            m_ref[hh, :, cols(j)] = m[j]
            l_ref[hh, :, cols(j)] = l[j]
            acc_ref[hh, :, cols(j)] = acc[j]

    def score_stage(diagonal):
        for hh in range(ATT_HEADS):
            mx = [None] * nsub
            for c in range(nsub):
                j0 = c if diagonal else 0
                st = _dot_nt(k_ref[c * sub:(c + 1) * sub, head_cols(hh, HEAD_SLOT)],
                             q_ref[j0 * sub:, head_cols(hh, HEAD_SLOT)])
                for j in range(j0, nsub):
                    sj = st[:, (j - j0) * sub:(j - j0 + 1) * sub]
                    if diagonal and j == c:
                        kc = lax.broadcasted_iota(jnp.int32, sj.shape, 0) // CHUNK
                        qc = lax.broadcasted_iota(jnp.int32, sj.shape, 1) // CHUNK
                        sj = jnp.where(kc <= qc, sj, -jnp.inf)
                    s_ref[hh, c * sub:(c + 1) * sub, cols(j)] = sj
                    cm = jnp.max(sj, axis=0, keepdims=True)
                    mx[j] = cm if mx[j] is None else jnp.maximum(mx[j], cm)
            for j in range(nsub):
                mx_ref[hh, :, cols(j)] = mx[j]

    def value_stage(diagonal):
        for hh in range(ATT_HEADS):
            m_old, l, acc = load_state(hh)
            m = [jnp.maximum(m_old[j], mx_ref[hh, :, cols(j)]) for j in range(nsub)]
            alpha = [jnp.exp2(m_old[j] - m[j]) for j in range(nsub)]
            l = [alpha[j] * l[j] for j in range(nsub)]
            acc = [alpha[j] * acc[j] for j in range(nsub)]
            for c in range(nsub):
                j0 = c if diagonal else 0
                vt = vt_ref[head_cols(hh, V_DIM), c * sub:(c + 1) * sub]
                for j in range(j0, nsub):
                    p = jnp.exp2(s_ref[hh, c * sub:(c + 1) * sub, cols(j)] - m[j])
                    l[j] = l[j] + jnp.sum(p, axis=0, keepdims=True)
                    acc[j] = acc[j] + _dot(vt, p.astype(BF16))
            store_state(hh, m, l, acc)

    def attend(key_blocks, diagonal):
        scores, maxes = [], []
        for hh in range(ATT_HEADS):
            m, _, _ = load_state(hh)
            st_h = []
            for c, (keys, _, mask) in enumerate(key_blocks):
                j0 = c if diagonal else 0
                st = _dot_nt(keys(hh), q_ref[j0 * sub:, head_cols(hh, HEAD_SLOT)])
                pieces = []
                for j in range(j0, nsub):
                    sj = st[:, (j - j0) * sub:(j - j0 + 1) * sub]
                    if mask is not None:
                        sj = mask(sj)
                    elif diagonal and j == c:
                        kc = lax.broadcasted_iota(jnp.int32, sj.shape, 0) // CHUNK
                        qc = lax.broadcasted_iota(jnp.int32, sj.shape, 1) // CHUNK
                        sj = jnp.where(kc <= qc, sj, -jnp.inf)
                    m[j] = jnp.maximum(m[j], jnp.max(sj, axis=0, keepdims=True))
                    pieces.append(sj)
                st_h.append(pieces)
            scores.append(st_h)
            maxes.append(m)
        for hh in range(ATT_HEADS):
            m_old, l, acc = load_state(hh)
            m = maxes[hh]
            alpha = [jnp.exp2(m_old[j] - m[j]) for j in range(nsub)]
            l = [alpha[j] * l[j] for j in range(nsub)]
            acc = [alpha[j] * acc[j] for j in range(nsub)]
            for c, (_, values, _) in enumerate(key_blocks):
                j0 = c if diagonal else 0
                vt = values(hh)
                for j in range(j0, nsub):
                    p = jnp.exp2(scores[hh][c][j - j0] - m[j])
                    l[j] = l[j] + jnp.sum(p, axis=0, keepdims=True)
                    acc[j] = acc[j] + _dot(vt, p.astype(BF16))
            store_state(hh, m, l, acc)

    @pl.when(kj == 0)
    def _():
        for hh in range(ATT_HEADS):
            m_ref[hh] = jnp.full(m_ref.shape[1:], -jnp.inf, F32)
            l_ref[hh] = jnp.zeros(l_ref.shape[1:], F32)
            acc_ref[hh] = jnp.zeros(acc_ref.shape[1:], F32)

        def meta_mask(st):
            key = lax.broadcasted_iota(jnp.int32, st.shape, 0)
            return jnp.where(key < N_META, st, -jnp.inf)

        attend([(lambda hh: km_ref[:, head_cols(hh, HEAD_SLOT)],
                 lambda hh: vmt_ref[head_cols(hh, V_DIM), :], meta_mask)], False)

    @pl.when(kj < qi)
    def _():
        score_stage(False)

    @pl.when(kj < qi)
    def _():
        value_stage(False)

    @pl.when(kj == qi)
    def _():
        score_stage(True)

    @pl.when(kj == qi)
    def _():
        value_stage(True)
        for hh in range(ATT_HEADS):
            for j in range(nsub):
                o = acc_ref[hh, :, cols(j)] / l_ref[hh, :, cols(j)]
                o_ref[cols(j), head_cols(hh, V_DIM)] = o.T.astype(o_ref.dtype)


def _attention(q, k, vt, bsz, seq):
    r = bsz * seq
    bq = BQ
    nq = seq // bq
    nh = ATT_HEADS
    pairs = [(i, j) for i in range(nq) for j in range(i + 1)]
    qi = jnp.asarray([p[0] for p in pairs], jnp.int32)
    kj = jnp.asarray([p[1] for p in pairs], jnp.int32)
    meta_blk = (bsz * seq) // LANES
    grid_spec = pltpu.PrefetchScalarGridSpec(
        num_scalar_prefetch=2,
        grid=(bsz, HEADS // nh, len(pairs)),
        in_specs=[
            pl.BlockSpec((bq, nh * HEAD_SLOT), lambda b, h, s, qi, kj: (b * nq + qi[s], h)),
            pl.BlockSpec((bq, nh * HEAD_SLOT), lambda b, h, s, qi, kj: (b * nq + kj[s], h)),
            pl.BlockSpec((nh * V_DIM, bq), lambda b, h, s, qi, kj: (h, b * nq + kj[s])),
            pl.BlockSpec((LANES, nh * HEAD_SLOT), lambda b, h, s, qi, kj: (meta_blk, h)),
            pl.BlockSpec((nh * V_DIM, LANES), lambda b, h, s, qi, kj: (h, meta_blk)),
        ],
        out_specs=pl.BlockSpec((bq, nh * V_DIM), lambda b, h, s, qi, kj: (b * nq + qi[s], h)),
        scratch_shapes=[pltpu.VMEM((nh, 1, bq), F32), pltpu.VMEM((nh, 1, bq), F32),
                        pltpu.VMEM((nh, V_DIM, bq), F32),
                        pltpu.VMEM((nh, bq, bq), F32), pltpu.VMEM((nh, 1, bq), F32)],
    )
    return pl.pallas_call(
        _attn_kernel,
        grid_spec=grid_spec,
        out_shape=jax.ShapeDtypeStruct((r, HEADS * V_DIM), BF16),
        compiler_params=_cparams(("parallel", "parallel", "arbitrary")),
        name="mla_attn",
    )(qi, kj, q, k, vt, k, vt)


def _attn_meta_kernel(q_ref, km_ref, vmt_ref, o_ref):
    s = _dot_nt(q_ref[...], km_ref[...])
    lane = lax.broadcasted_iota(jnp.int32, s.shape, 1)
    s = jnp.where(lane < N_META, s, -jnp.inf)
    m = jnp.max(s, axis=-1, keepdims=True)
    p = jnp.exp2(s - m)
    l = jnp.sum(p, axis=-1, keepdims=True)
    o_ref[...] = (_dot_nt(p.astype(BF16), vmt_ref[...]) / l).astype(o_ref.dtype)


def _attention_meta(q, k, vt, rows_real):
    meta_blk = rows_real // LANES
    tile_blk = rows_real // META_TILE
    return pl.pallas_call(
        _attn_meta_kernel,
        grid=(HEADS,),
        in_specs=[pl.BlockSpec((META_TILE, HEAD_SLOT), lambda h: (tile_blk, h)),
                  pl.BlockSpec((LANES, HEAD_SLOT), lambda h: (meta_blk, h)),
                  pl.BlockSpec((V_DIM, LANES), lambda h: (h, meta_blk))],
        out_specs=pl.BlockSpec((META_TILE, V_DIM), lambda h: (0, h)),
        out_shape=jax.ShapeDtypeStruct((META_TILE, HEADS * V_DIM), BF16),
        compiler_params=_cparams(("parallel",)),
        name="mla_attn_meta",
    )(q, k, vt)


CONV_RB = 64
CONV_LB = 256


def _conv_kernel(val_ref, gate_ref, hval_ref, hgate_ref, mval_ref, mgate_ref, w_ref, cb_ref,
                 g_ref, b_ref, o_ref, z_ref, c_ref, *, tt, tiles_per_seq, n_real_tiles):
    i = pl.program_id(0)
    chans = o_ref.shape[1]

    def glu(v, g):
        return v.astype(F32) * jax.nn.sigmoid(g.astype(F32))

    z_ref[CONV_HALO:CONV_HALO + tt, :] = glu(val_ref[...], gate_ref[...])
    z_ref[CONV_HALO + tt:, :] = jnp.zeros((8, chans), F32)
    is_meta = i >= n_real_tiles
    first = jnp.logical_and(i % tiles_per_seq == 0, jnp.logical_not(is_meta))
    z_prev = glu(hval_ref[...], hgate_ref[...])
    z_meta = glu(mval_ref[...], mgate_ref[...])
    pad = CONV_HALO - N_META
    z_start = jnp.concatenate([jnp.zeros((pad, chans), F32), z_meta[:N_META]], axis=0)
    z_ref[:CONV_HALO, :] = jnp.where(is_meta, 0.0, jnp.where(first, z_start, z_prev))

    off = CONV_HALO - (CONV_K - 1)
    for rb in range(tt // CONV_RB):
        for lb in range(chans // CONV_LB):
            cs = slice(lb * CONV_LB, (lb + 1) * CONV_LB)
            acc = jnp.zeros((CONV_RB, CONV_LB), F32)
            for res in range(8):
                taps = [k for k in range(CONV_K) if (off + k) % 8 == res]
                base = rb * CONV_RB + (off + taps[0]) - res
                part = jnp.zeros((CONV_RB + 8, CONV_LB), F32)
                for k in taps:
                    r0 = base + (k - taps[0])
                    wk = jnp.tile(w_ref[8 * k:8 * k + 8, cs], (CONV_RB // 8 + 1, 1))
                    part = part + z_ref[r0:r0 + CONV_RB + 8, cs] * wk
                acc = acc + part[res:res + CONV_RB]
            c_ref[rb * CONV_RB:(rb + 1) * CONV_RB, cs] = acc + cb_ref[:, cs]
    y = _layer_norm(c_ref[...], g_ref[...], b_ref[...])
    o_ref[...] = (y * jax.nn.sigmoid(y)).astype(o_ref.dtype)


def _conv_branch(pc, conv_w, conv_b, ln_g, ln_b, rows_real, seq):
    r = pc.shape[0]
    chans = conv_w.shape[1]
    tt = TT_CONV
    hb = tt // CONV_HALO
    kern = functools.partial(_conv_kernel, tt=tt, tiles_per_seq=seq // tt,
                             n_real_tiles=rows_real // tt)
    meta_blk = rows_real // CONV_HALO

    def halo(i):
        return jnp.maximum(i * hb - 1, 0)

    row = lambda a: a.reshape(1, chans)
    return pl.pallas_call(
        kern,
        grid=(r // tt,),
        in_specs=[pl.BlockSpec((tt, chans), lambda i: (i, 0)),
                  pl.BlockSpec((tt, chans), lambda i: (i, 1)),
                  pl.BlockSpec((CONV_HALO, chans), lambda i: (halo(i), 0)),
                  pl.BlockSpec((CONV_HALO, chans), lambda i: (halo(i), 1)),
                  pl.BlockSpec((CONV_HALO, chans), lambda i: (meta_blk, 0)),
                  pl.BlockSpec((CONV_HALO, chans), lambda i: (meta_blk, 1)),
                  pl.BlockSpec((8 * CONV_K, chans), lambda i: (0, 0)),
                  pl.BlockSpec((1, chans), lambda i: (0, 0)),
                  pl.BlockSpec((1, chans), lambda i: (0, 0)),
                  pl.BlockSpec((1, chans), lambda i: (0, 0))],
        out_specs=pl.BlockSpec((tt, chans), lambda i: (i, 0)),
        out_shape=jax.ShapeDtypeStruct((r, chans), BF16),
        scratch_shapes=[pltpu.VMEM((CONV_HALO + tt + 8, chans), F32), pltpu.VMEM((tt, chans), F32)],
        compiler_params=_cparams(("parallel",)),
        name="conv_branch",
    )(pc, pc, pc, pc, pc, pc, jnp.repeat(conv_w, 8, axis=0), row(conv_b), row(ln_g), row(ln_b))


SSM_SL = SSM_CHUNK * SSM_GROUP


def _ssm_prep_kernel(lc_ref, lr_ref, b_ref, c_ref, cc_ref, e_ref, et_ref, es_ref, est_ref, tt_ref,
                     pt_ref, qst_ref, sc_ref, *, n_steps):
    ns, ch = SSM_STATE, SSM_CHUNK

    def power(lre_dt, ang, e):
        mag = jnp.exp(lre_dt * e)
        return mag * jnp.cos(ang * e), mag * jnp.sin(ang * e)

    lc = lc_ref[0]
    lam_re, lam_im = lc[:, 0:1], lc[:, 1:2]
    dt = jnp.exp(lc[:, 2:3])
    lre_dt, ang = lam_re * dt, lam_im * dt
    a_re, a_im = power(lre_dt, ang, 1.0)
    den = lam_re * lam_re + lam_im * lam_im
    f_re = (((a_re - 1.0) * lam_re + a_im * lam_im) / den)[:ns]
    f_im = ((a_im * lam_re - (a_re - 1.0) * lam_im) / den)[:ns]
    lane1 = lax.broadcasted_iota(jnp.int32, (1, LANES), 1)
    back = jnp.maximum(ch - 1 - lane1, 0).astype(F32)
    ac_re, ac_im = power(lre_dt[:ns], ang[:ns], back)
    ae_re = _dot_exact(ac_re, es_ref[...])
    ae_im = _dot_exact(ac_im, es_ref[...])
    expand = e_ref[...]
    b_re = _dot_exact(b_ref[0, 0], expand)
    b_im = _dot_exact(b_ref[0, 1], expand)
    bb_re = f_re * b_re - f_im * b_im
    bb_im = f_re * b_im + f_im * b_re
    p_re = ae_re * bb_re - ae_im * bb_im
    p_im = ae_re * bb_im + ae_im * bb_re
    pt_ref[0, :ns, :] = p_re.astype(pt_ref.dtype)
    pt_ref[0, ns:, :] = p_im.astype(pt_ref.dtype)
    rev = _dot_exact(c_ref[0, 0], p_re) - _dot_exact(c_ref[0, 1], p_im)
    lane_t = lax.broadcasted_iota(jnp.int32, rev.shape, 1)
    for t in range(ch):
        w = (t + 1) * SSM_GROUP
        rolled = rev if w == SSM_SL else pltpu.roll(rev, w, 1)
        tt_ref[0, t * SSM_GROUP:(t + 1) * SSM_GROUP, :] = jnp.where(
            lane_t < w, rolled, 0.0).astype(tt_ref.dtype)
    lane2 = lax.broadcasted_iota(jnp.int32, (LANES, LANES), 1)
    row2 = lax.broadcasted_iota(jnp.int32, (LANES, LANES), 0)
    expo = (ch * jnp.left_shift(1, jnp.minimum(lane2[0:1] // 2, n_steps))).astype(F32)
    s_re, s_im = power(lre_dt, ang, expo)
    sc_ref[0] = jnp.where(lane2 % 2 == 0, s_re, jnp.where(row2 < ns, -s_im, s_im))

    lr = lr_ref[0]
    dt_r = jnp.exp(lr[2:3])
    tp1 = (lax.broadcasted_iota(jnp.int32, (ch, 1), 0) + 1).astype(F32)
    at_re, at_im = power(lr[0:1] * dt_r, lr[1:2] * dt_r, tp1)
    ap_re = _dot_exact(est_ref[...], at_re)
    ap_im = _dot_exact(est_ref[...], at_im)
    ctile = _dot_exact(et_ref[...], cc_ref[0])
    sign = jnp.where(lane2[0:1] < ns, 1.0, -1.0)
    qst_ref[0] = (sign * (ap_re * ctile) - pltpu.roll(ap_im * ctile, ns, 1)).astype(qst_ref.dtype)


def _ssm_prep(lam_re, lam_im, log_dt, b_re, b_im, c_re, c_im, n_steps):
    nl, ng, ns = lam_re.shape
    lg = nl * ng
    dtb = jnp.broadcast_to(log_dt[..., None], lam_re.shape)
    lrow = jnp.stack([lam_re, lam_im, dtb], axis=-2)
    lrow = jnp.concatenate([lrow, lrow], axis=-1).reshape(lg, 3, 2 * ns)
    lcol = jnp.swapaxes(lrow, -1, -2)
    b = jnp.stack([b_re, b_im], axis=2).reshape(lg, 2, ns, SSM_GROUP)
    c = jnp.stack([c_re, c_im], axis=2).reshape(lg, 2, SSM_GROUP, ns)
    ccat = jnp.concatenate([c_re, c_im], axis=-1).reshape(lg, SSM_GROUP, 2 * ns)
    lane_c = jnp.arange(SSM_SL) % SSM_GROUP
    expand = (lane_c[None, :] == jnp.arange(SSM_GROUP)[:, None]).astype(F32)
    lane_s = jnp.arange(SSM_SL) // SSM_GROUP
    expand_step = (lane_s[None, :] == jnp.arange(LANES)[:, None]).astype(F32)
    expand_step_t = (lane_s[:, None] == jnp.arange(SSM_CHUNK)[None, :]).astype(F32)
    const = lambda a: pl.BlockSpec(a.shape, lambda i: (0, 0))
    blk = lambda *shape: pl.BlockSpec((1,) + shape, lambda i: (i,) + (0,) * len(shape))
    return pl.pallas_call(
        functools.partial(_ssm_prep_kernel, n_steps=n_steps),
        grid=(lg,),
        in_specs=[blk(2 * ns, 3), blk(3, 2 * ns), blk(2, ns, SSM_GROUP), blk(2, SSM_GROUP, ns),
                  blk(SSM_GROUP, 2 * ns),
                  const(expand), const(expand.T), const(expand_step), const(expand_step_t)],
        out_specs=[blk(SSM_SL, SSM_SL), blk(2 * ns, SSM_SL), blk(SSM_SL, 2 * ns), blk(2 * ns, 2 * ns)],
        out_shape=[jax.ShapeDtypeStruct((lg, SSM_SL, SSM_SL), BF16),
                   jax.ShapeDtypeStruct((lg, 2 * ns, SSM_SL), BF16),
                   jax.ShapeDtypeStruct((lg, SSM_SL, 2 * ns), BF16),
                   jax.ShapeDtypeStruct((lg, 2 * ns, 2 * ns), F32)],
        compiler_params=_cparams(("parallel",)),
        name="ssm_prep",
    )(lcol, lrow, b, c, ccat, expand, expand.T, expand_step, expand_step_t)


def _swap_halves(x):
    return pltpu.roll(x, SSM_STATE, 0)


def _ssm_kernel(u_ref, tt_ref, pt_ref, qst_ref, sc_ref, d_ref, um_ref, o_ref, ym_ref, ut_ref, yt_ref,
                us_ref, *, nk, n_steps):
    ch = SSM_CHUNK
    for s in range(ch):
        us = u_ref[pl.ds(s, nk, stride=ch), :]
        us_ref[s] = us
        ust = us.T
        for g in range(SSM_GB):
            ut_ref[g, s * SSM_GROUP:(s + 1) * SSM_GROUP, :] = (
                ust[g * SSM_GROUP:(g + 1) * SSM_GROUP, :].astype(BF16))
    lane = lax.broadcasted_iota(jnp.int32, (LANES, nk), 1)
    for g in range(SSM_GB):
        ut = ut_ref[g]
        um = um_ref[g]
        tt, pt, sc = tt_ref[g], pt_ref[g], sc_ref[g]
        ym_ref[g] = _dot_nt(um, tt)[:8]
        g0 = _dot_nt(pt, um)[:, 0:1]
        seed = sc[:, 0:1] * g0 + sc[:, 1:2] * _swap_halves(g0)
        e = _dot(pt, ut) + jnp.where(lane == 0, seed, 0.0)
        for i in range(n_steps):
            sh = jnp.where(lane >= 2 ** i, pltpu.roll(e, 2 ** i, 1), 0.0)
            e = e + sc[:, 2 * i:2 * i + 1] * sh + sc[:, 2 * i + 1:2 * i + 2] * _swap_halves(sh)
        state_in = jnp.where(lane == 0, g0, pltpu.roll(e, 1, 1))
        yt_ref[g] = _dot(tt, ut) + _dot(qst_ref[g], state_in.astype(BF16))
    for t in range(ch):
        yt = jnp.concatenate([yt_ref[g, t * SSM_GROUP:(t + 1) * SSM_GROUP, :]
                              for g in range(SSM_GB)], axis=0)
        y = yt.T + d_ref[...] * us_ref[t]
        o_ref[pl.ds(t, nk, stride=ch), :] = jax.nn.gelu(y, approximate=True)


def _ssm_meta_kernel(ym_ref, um_ref, d_ref, o_ref):
    o_ref[N_META:, :] = jnp.zeros((o_ref.shape[0] - N_META, o_ref.shape[1]), o_ref.dtype)
    y = ym_ref[...] + d_ref[...] * um_ref[...]
    o_ref[:N_META, :] = jax.nn.gelu(y, approximate=True)


def _ssm_branch(u_nat, prep, d_row, layer, bsz, seq):
    tt_all, pt_all, qst_all, sc_all = prep
    chans = u_nat.shape[1]
    ng = chans // SSM_GROUP
    nj = ng // SSM_GB
    rows_real = bsz * seq
    ch = SSM_CHUNK
    nk = seq // ch
    n_steps = nk.bit_length() - 1
    assert nk == 2 ** n_steps and nk % LANES == 0 and ch >= N_META
    u_meta = u_nat[rows_real:rows_real + N_META]
    um = u_meta.reshape(N_META, ng, SSM_GROUP).transpose(1, 0, 2).reshape(ng, N_META * SSM_GROUP)
    um = jnp.pad(um, ((0, 0), ((ch - N_META) * SSM_GROUP, 0)))
    um = jnp.pad(um[:, None, :], ((0, 0), (0, LANES - 1), (0, 0))).astype(BF16)
    j0 = layer * nj
    blk = lambda *shape: pl.BlockSpec((SSM_GB,) + shape, lambda j, b: (j0 + j,) + (0,) * len(shape))
    kern = functools.partial(_ssm_kernel, nk=nk, n_steps=n_steps)
    gy, ym = pl.pallas_call(
        kern,
        grid=(nj, bsz),
        in_specs=[pl.BlockSpec((seq, LANES), lambda j, b: (b, j)),
                  blk(SSM_SL, SSM_SL), blk(LANES, SSM_SL), blk(SSM_SL, LANES), blk(LANES, LANES),
                  pl.BlockSpec((1, LANES), lambda j, b: (0, j)),
                  pl.BlockSpec((SSM_GB, LANES, SSM_SL), lambda j, b: (j, 0, 0))],
        out_specs=[pl.BlockSpec((seq, LANES), lambda j, b: (b, j)),
                   pl.BlockSpec((SSM_GB, 8, SSM_SL), lambda j, b: (j, 0, 0))],
        out_shape=[jax.ShapeDtypeStruct((rows_real, chans), F32),
                   jax.ShapeDtypeStruct((ng, 8, SSM_SL), F32)],
        scratch_shapes=[pltpu.VMEM((SSM_GB, SSM_SL, nk), BF16), pltpu.VMEM((SSM_GB, SSM_SL, nk), F32),
                        pltpu.VMEM((ch, nk, LANES), F32)],
        compiler_params=_cparams(("parallel", "arbitrary")),
        name="ssm_scan",
    )(u_nat, tt_all, pt_all, qst_all, sc_all, d_row, um)
    y_meta = ym[:, 0, :].reshape(ng, ch, SSM_GROUP)[:, ch - N_META:, :]
    y_meta = y_meta.transpose(1, 0, 2).reshape(N_META, chans)
    gy_meta = pl.pallas_call(
        _ssm_meta_kernel,
        grid=(1,),
        in_specs=[pl.BlockSpec((N_META, chans), lambda i: (0, 0)),
                  pl.BlockSpec((N_META, chans), lambda i: (0, 0)),
                  pl.BlockSpec((1, chans), lambda i: (0, 0))],
        out_specs=pl.BlockSpec((META_TILE, chans), lambda i: (0, 0)),
        out_shape=jax.ShapeDtypeStruct((META_TILE, chans), F32),
        compiler_params=_cparams(("arbitrary",)),
        name="ssm_meta",
    )(y_meta, u_meta, d_row)
    return gy, gy_meta


def _mix_kernel(o_ref, om_ref, zc_ref, gy_ref, gym_ref, ga_ref, gb_ref, gc_ref, wa_ref, wb_ref,
                wza_ref, wzb_ref, out_ref, *, n_real):
    is_meta = pl.program_id(1) == n_real
    ya = _dot(jnp.where(is_meta, om_ref[...], o_ref[...]), wa_ref[...])
    yb = _dot(zc_ref[...], wb_ref[...])
    gy = jnp.where(is_meta, gym_ref[...], gy_ref[...]).astype(BF16)
    yc = _dot(gy, wza_ref[...]) * jax.nn.sigmoid(_dot(gy, wzb_ref[...]))
    mix = (ga_ref[...].astype(F32) * ya + gb_ref[...].astype(F32) * yb
           + gc_ref[...].astype(F32) * yc)
    out_ref[...] = mix.astype(out_ref.dtype)


def _mix(o, o_meta, zc, gy, gy_meta, gates, w_mla_out, w_conv_out, w_glu):
    r, d = zc.shape
    tm, tn = META_TILE, 512
    n_real = o.shape[0] // tm
    assert o.shape[0] % tm == 0 and r == o.shape[0] + META_TILE
    nb = d // tn
    act = lambda: pl.BlockSpec((tm, d), lambda j, i: (i, 0))
    real = lambda: pl.BlockSpec((tm, d), lambda j, i: (jnp.minimum(i, n_real - 1), 0))
    meta = lambda: pl.BlockSpec((tm, d), lambda j, i: (0, 0))
    gate = lambda k: pl.BlockSpec((tm, tn), lambda j, i: (i, k * nb + j))
    wgt = lambda k: pl.BlockSpec((d, tn), lambda j, i: (0, k * nb + j))
    return pl.pallas_call(
        functools.partial(_mix_kernel, n_real=n_real),
        grid=(nb, r // tm),
        in_specs=[real(), meta(), act(), real(), meta(), gate(0), gate(1), gate(2),
                  wgt(0), wgt(0), wgt(0), wgt(1)],
        out_specs=pl.BlockSpec((tm, tn), lambda j, i: (i, j)),
        out_shape=jax.ShapeDtypeStruct((r, d), BF16),
        compiler_params=_cparams(("parallel", "parallel")),
        name="gated_mix",
    )(o, o_meta, zc, gy, gy_meta, gates, gates, gates, w_mla_out, w_conv_out, w_glu, w_glu)


def _mm_res_ln_kernel(x_ref, w_ref, r_ref, g_ref, b_ref, h_ref, hb_ref, acc_ref, *, alpha, nk):
    kk = pl.program_id(1)

    @pl.when(kk == 0)
    def _():
        acc_ref[...] = jnp.zeros_like(acc_ref)

    acc_ref[...] += _dot(x_ref[...], w_ref[...])

    @pl.when(kk == nk - 1)
    def _():
        y = _layer_norm(alpha * r_ref[...] + acc_ref[...], g_ref[...], b_ref[...])
        h_ref[...] = y
        hb_ref[...] = y.astype(BF16)


def _mm_res_ln(x, w, res, g, b, *, alpha, tk, tm_candidates=(TM_LN,)):
    r, k = x.shape
    d = w.shape[1]
    tm = _row_tile(r, tm_candidates)
    nk = k // tk
    kern = functools.partial(_mm_res_ln_kernel, alpha=alpha, nk=nk)
    return pl.pallas_call(
        kern,
        grid=(r // tm, nk),
        in_specs=[pl.BlockSpec((tm, tk), lambda i, kk: (i, kk)),
                  pl.BlockSpec((tk, d), lambda i, kk: (kk, 0)),
                  pl.BlockSpec((tm, d), lambda i, kk: (i, 0)),
                  pl.BlockSpec((1, d), lambda i, kk: (0, 0)),
                  pl.BlockSpec((1, d), lambda i, kk: (0, 0))],
        out_specs=[pl.BlockSpec((tm, d), lambda i, kk: (i, 0)),
                   pl.BlockSpec((tm, d), lambda i, kk: (i, 0))],
        out_shape=[jax.ShapeDtypeStruct((r, d), F32), jax.ShapeDtypeStruct((r, d), BF16)],
        scratch_shapes=[pltpu.VMEM((tm, d), F32)],
        compiler_params=_cparams(("parallel", "arbitrary")),
        name="mm_res_ln",
    )(x, w, res, g.reshape(1, d), b.reshape(1, d))


def _ffn_up_kernel(x_ref, wg_ref, wu_ref, o_ref):
    x = x_ref[...]
    a = _dot(x, wg_ref[...])
    o_ref[...] = (a * jax.nn.sigmoid(a) * _dot(x, wu_ref[...])).astype(o_ref.dtype)


def _ffn_up(x, wg, wu):
    r, d = x.shape
    f = wg.shape[1]
    tm, tn = _row_tile(r, TM_MM), 512
    return pl.pallas_call(
        _ffn_up_kernel,
        grid=(f // tn, r // tm),
        in_specs=[pl.BlockSpec((tm, d), lambda j, i: (i, 0)),
                  pl.BlockSpec((d, tn), lambda j, i: (0, j)),
                  pl.BlockSpec((d, tn), lambda j, i: (0, j))],
        out_specs=pl.BlockSpec((tm, tn), lambda j, i: (i, j)),
        out_shape=jax.ShapeDtypeStruct((r, f), BF16),
        compiler_params=_cparams(("parallel", "parallel")),
        name="ffn_up",
    )(x, wg, wu)


def _rope_table(positions, scale):
    half = QK_ROPE // 2
    inv = jnp.power(ROPE_BASE, -jnp.arange(half, dtype=F32) / half)
    ang = positions.astype(F32)[:, None] * inv[None, :]
    cos, sin = jnp.cos(ang), jnp.sin(ang)
    return jnp.concatenate([cos, cos, -sin, sin], axis=-1) * scale


def _rope_cols(w):
    half = QK_ROPE // 2
    x1, x2 = w[:, :half], w[:, half:]
    return jnp.concatenate([x1, x2, x2, x1], axis=1)


def _split_w_in(w, d):
    qk = QK_NOPE + QK_ROPE
    o_q = HEADS * qk
    o_ckv = o_q + KV_RANK
    o_kr = o_ckv + QK_ROPE
    o_conv = o_kr + 2 * d
    o_u = o_conv + d
    wq = w[:, :o_q].reshape(d, HEADS, qk)
    half = QK_ROPE // 2
    x1, x2 = wq[:, :, QK_NOPE:QK_NOPE + half], wq[:, :, QK_NOPE + half:]
    w_q = jnp.concatenate([wq[:, :, :QK_NOPE], x1, x2, x2, x1], axis=2).reshape(d, HEADS * HEAD_SLOT)
    w_ckv = jnp.concatenate([w[:, o_q:o_ckv], _rope_cols(w[:, o_ckv:o_kr])], axis=1)
    w_conv = w[:, o_kr:o_conv]
    w_u = w[:, o_conv:o_u]
    w_g = w[:, o_u:]
    return tuple(a.astype(BF16) for a in (w_q, w_ckv, w_conv, w_u, w_g))


def _split_w_ukv(w):
    wr = w.reshape(KV_RANK, HEADS, QK_NOPE + V_DIM)
    w_k = wr[:, :, :QK_NOPE].reshape(KV_RANK, HEADS * QK_NOPE)
    w_vt = wr[:, :, QK_NOPE:].reshape(KV_RANK, HEADS * V_DIM).T
    return w_k.astype(BF16), w_vt.astype(BF16)


def kernel(x, meta_tokens, ln0_g, ln0_b, w_in, b_gate, kv_norm_g, w_ukv, w_mla_out, conv_w, conv_b, conv_ln_g, conv_ln_b, w_conv_out, ssm_lam_re, ssm_lam_im, ssm_log_dt, ssm_b_re, ssm_b_im, ssm_c_re, ssm_c_im, ssm_d, w_glu, w_out, ln1_g, ln1_b, w_ffn_gate, w_ffn_up, w_ffn_down, ln2_g, ln2_b):
    bsz, seq, d = x.shape
    depth = w_in.shape[0]
    rows_real = bsz * seq
    assert seq % BQ == 0 and seq % TT_CONV == 0 and META_TILE % TM_LN == 0
    alpha = (2 * depth) ** 0.25

    x_meta = jnp.concatenate([meta_tokens.astype(x.dtype),
                              jnp.zeros((META_TILE - N_META, d), x.dtype)], axis=0)
    pos = jnp.concatenate([jnp.tile(N_META + jnp.arange(seq), bsz), jnp.arange(META_TILE)])
    q_tbl = _rope_table(pos, Q_SCALE)
    k_tbl = _rope_table(pos, 1.0)
    n_scan = (seq // SSM_CHUNK).bit_length() - 1
    prep = _ssm_prep(ssm_lam_re, ssm_lam_im, ssm_log_dt, ssm_b_re, ssm_b_im, ssm_c_re, ssm_c_im,
                     n_scan)

    h, hb = _ln0(x.reshape(rows_real, d), x_meta, ln0_g, ln0_b)
    for l in range(depth):
        w_q, w_ckv, w_conv, w_u, w_g = _split_w_in(w_in[l], d)
        q = _matmul(hb, w_q, tn=4 * HEAD_SLOT, rope_tbl=q_tbl, name="in_proj_q")
        ckv = _matmul(hb, w_ckv, tn=CKV_COLS, name="in_proj_ckv")
        pc = _matmul(hb, w_conv, tn=1024, name="in_proj_conv")
        u_nat = _matmul(hb, w_u, tn=1024, out_dtype=F32, name="in_proj_u")
        gates = _matmul(hb, w_g, tn=1024, bias=b_gate[l], name="in_proj_gates")

        w_k, w_vt = _split_w_ukv(w_ukv[l])
        k, vt = _kv_up(ckv, kv_norm_g[l], w_k, w_vt, k_tbl)
        o = _attention(q, k, vt, bsz, seq)
        o_meta = _attention_meta(q, k, vt, rows_real)
        zc = _conv_branch(pc, conv_w[l], conv_b[l], conv_ln_g[l], conv_ln_b[l], rows_real, seq)
        gy, gy_meta = _ssm_branch(u_nat, prep, ssm_d[l:l + 1], l, bsz, seq)

        mix = _mix(o, o_meta, zc, gy, gy_meta, gates, w_mla_out[l].astype(BF16),
                   w_conv_out[l].astype(BF16), w_glu[l].astype(BF16))
        h, hb = _mm_res_ln(mix, w_out[l].astype(BF16), h, ln1_g[l], ln1_b[l], alpha=alpha, tk=d)
        ff = _ffn_up(hb, w_ffn_gate[l].astype(BF16), w_ffn_up[l].astype(BF16))
        h, hb = _mm_res_ln(ff, w_ffn_down[l].astype(BF16), h, ln2_g[l], ln2_b[l], alpha=alpha,
                           tk=w_ffn_down.shape[1] // 4, tm_candidates=TM_FFN_DOWN)
    return h[:rows_real].reshape(bsz, seq, d)
```
